```python
import math, functools
import jax, jax.numpy as jnp
from jax import lax
import numpy as np

D_MODEL = 2048
BATCH = 2
SEQ = 4096
DEPTH = 1

CHUNK = 64
D_LRU = 1024
LRU_HEADS = 16
LRU_HEAD_DIM = D_LRU // LRU_HEADS
LRU_CONV = 4
LRU_C = 8.0
D_SC = 1024
SC_CONV = 3
D_FF = 5632
FFN_CONV = 3
N_BRANCH = 2
EPS = 1e-6
IN_COLS = 2 * D_LRU + 3 * D_SC + N_BRANCH * D_MODEL

kernel_name = "hybrid_rglru_shortconv_convffn_block"


def rmsnorm(x, g):
    xf = x.astype(jnp.float32)
    y = xf * lax.rsqrt(jnp.mean(xf * xf, axis=-1, keepdims=True) + EPS)
    return (y * g.astype(jnp.float32)).astype(x.dtype)


def causal_dwconv(x, w):
    k_w = w.shape[0]
    s = x.shape[1]
    xp = jnp.pad(x, ((0, 0), (k_w - 1, 0), (0, 0)))
    y = xp[:, 0:s] * w[0]
    for k in range(1, k_w):
        y = y + xp[:, k:k + s] * w[k]
    return y


def _lin_combine(left, right):
    a1, b1 = left
    a2, b2 = right
    return a1 * a2, a2 * b1 + b2


def rg_lru(x, w_a, b_a, w_x, b_x, lam):
    bsz, s, d = x.shape
    xf = x.astype(jnp.float32)
    xh = xf.reshape(bsz, s, LRU_HEADS, LRU_HEAD_DIM)
    r = jax.nn.sigmoid(jnp.einsum('bshi,hij->bshj', xh, w_a.astype(jnp.float32)).reshape(bsz, s, d) + b_a.astype(jnp.float32))
    i = jax.nn.sigmoid(jnp.einsum('bshi,hij->bshj', xh, w_x.astype(jnp.float32)).reshape(bsz, s, d) + b_x.astype(jnp.float32))
    log_a = -LRU_C * jax.nn.softplus(-lam.astype(jnp.float32)) * r
    a = jnp.exp(log_a)
    u = jnp.sqrt(-jnp.expm1(2.0 * log_a)) * (i * xf)
    n_chunks = s // CHUNK
    a_c = a.reshape(bsz, n_chunks, CHUNK, d).transpose(1, 0, 2, 3)
    u_c = u.reshape(bsz, n_chunks, CHUNK, d).transpose(1, 0, 2, 3)

    def step(h0, inp):
        ac, uc = inp
        a_cum, b_cum = lax.associative_scan(_lin_combine, (ac, uc), axis=1)
        h = a_cum * h0[:, None, :] + b_cum
        return h[:, -1], h

    h_init = jnp.zeros((bsz, d), jnp.float32)
    _, hs = lax.scan(step, h_init, (a_c, u_c))
    return hs.transpose(1, 0, 2, 3).reshape(bsz, s, d).astype(x.dtype)


def setup_inputs(seed: int = 0) -> dict:
    key = jax.random.key(seed)
    ks = jax.random.split(key, 20)
    f32 = jnp.float32
    nrm = lambda k, shp, fan: jax.random.normal(k, shp, f32) * (fan ** -0.5)
    a0 = jax.random.uniform(ks[9], (D_LRU,), f32, 0.9, 0.999)
    return {
        "x": jax.random.normal(ks[0], (BATCH, SEQ, D_MODEL), f32),
        "g_mix": 1.0 + 0.02 * jax.random.normal(ks[1], (D_MODEL,), f32),
        "w_in": nrm(ks[2], (D_MODEL, IN_COLS), D_MODEL),
        "lru_conv_w": nrm(ks[3], (LRU_CONV, D_LRU), LRU_CONV),
        "lru_conv_b": 0.01 * jax.random.normal(ks[4], (D_LRU,), f32),
        "lru_wa": nrm(ks[5], (LRU_HEADS, LRU_HEAD_DIM, LRU_HEAD_DIM), LRU_HEAD_DIM),
        "lru_ba": 0.01 * jax.random.normal(ks[6], (D_LRU,), f32),
        "lru_wx": nrm(ks[7], (LRU_HEADS, LRU_HEAD_DIM, LRU_HEAD_DIM), LRU_HEAD_DIM),
        "lru_bx": 0.01 * jax.random.normal(ks[8], (D_LRU,), f32),
        "lru_lambda": jnp.log(a0) - jnp.log1p(-a0),
        "lru_w_out": nrm(ks[10], (D_LRU, D_MODEL), D_LRU),
        "sc_conv_w": nrm(ks[11], (SC_CONV, D_SC), SC_CONV),
        "sc_w_out": nrm(ks[12], (D_SC, D_MODEL), D_SC),
        "w_o": nrm(ks[13], (D_MODEL, D_MODEL), D_MODEL),
        "g_ffn": 1.0 + 0.02 * jax.random.normal(ks[14], (D_MODEL,), f32),
        "ffn_w_up": nrm(ks[15], (D_MODEL, 2 * D_FF), D_MODEL),
        "ffn_conv_w": nrm(ks[16], (FFN_CONV, 2 * D_FF), FFN_CONV),
        "ffn_w_down": nrm(ks[17], (D_FF, D_MODEL), D_FF),
        "g_final": 1.0 + 0.02 * jax.random.normal(ks[18], (D_MODEL,), f32),
    }


def reference(x, g_mix, w_in, lru_conv_w, lru_conv_b, lru_wa, lru_ba, lru_wx, lru_bx,
              lru_lambda, lru_w_out, sc_conv_w, sc_w_out, w_o, g_ffn, ffn_w_up,
              ffn_conv_w, ffn_w_down, g_final):
    for _ in range(DEPTH):
        h = rmsnorm(x, g_mix)
        p = h @ w_in
        o = 0
        lru_x = p[..., o:o + D_LRU]; o += D_LRU
        lru_gate = p[..., o:o + D_LRU]; o += D_LRU
        sc_b = p[..., o:o + D_SC]; o += D_SC
        sc_c = p[..., o:o + D_SC]; o += D_SC
        sc_v = p[..., o:o + D_SC]; o += D_SC
        gate_lru = p[..., o:o + D_MODEL]; o += D_MODEL
        gate_sc = p[..., o:o + D_MODEL]

        xc = causal_dwconv(lru_x, lru_conv_w) + lru_conv_b
        y_lru = rg_lru(xc, lru_wa, lru_ba, lru_wx, lru_bx, lru_lambda)
        y_lru = (jax.nn.gelu(lru_gate) * y_lru) @ lru_w_out

        y_sc = (sc_b * causal_dwconv(sc_c * sc_v, sc_conv_w)) @ sc_w_out

        merged = jax.nn.sigmoid(gate_lru) * y_lru + jax.nn.sigmoid(gate_sc) * y_sc
        x = x + merged @ w_o

        h = rmsnorm(x, g_ffn)
        u = causal_dwconv(h @ ffn_w_up, ffn_conv_w)
        ff_gate, ff_val = u[..., :D_FF], u[..., D_FF:]
        x = x + (jax.nn.silu(ff_gate) * ff_val) @ ffn_w_down
    return rmsnorm(x, g_final)
```

```python
import functools

import jax
import jax.numpy as jnp
from jax import lax
from jax.experimental import pallas as pl
from jax.experimental.pallas import tpu as pltpu

EPS = 1e-6
LRU_C = 8.0
SUBLANES = 8
VMEM_LIMIT_BYTES = 56 * 1024 * 1024


def _rmsnorm(x, g):
    ms = jnp.mean(x * x, axis=-1, keepdims=True)
    return x * lax.rsqrt(ms + EPS) * g


def _in_proj_kernel(x_ref, g_ref, w_ref, o_ref, h_ref):
    @pl.when(pl.program_id(1) == 0)
    def _():
        h_ref[...] = _rmsnorm(x_ref[...], g_ref[...]).astype(jnp.bfloat16)

    o_ref[...] = jnp.dot(h_ref[...], w_ref[...], preferred_element_type=jnp.float32)


def _in_proj(x2, g, w_bf16, *, tm, tn):
    n, d = x2.shape
    cols = w_bf16.shape[1]
    return pl.pallas_call(
        _in_proj_kernel,
        grid=(n // tm, cols // tn),
        in_specs=[
            pl.BlockSpec((tm, d), lambda i, j: (i, 0)),
            pl.BlockSpec((1, d), lambda i, j: (0, 0)),
            pl.BlockSpec((d, tn), lambda i, j: (0, j)),
        ],
        out_specs=pl.BlockSpec((tm, tn), lambda i, j: (i, j)),
        out_shape=jax.ShapeDtypeStruct((n, cols), jnp.float32),
        scratch_shapes=[pltpu.VMEM((tm, d), jnp.bfloat16)],
        compiler_params=pltpu.CompilerParams(
            dimension_semantics=("arbitrary", "arbitrary"),
            vmem_limit_bytes=VMEM_LIMIT_BYTES),
        name="in_proj",
    )(x2, g, w_bf16)


def _gelu_tanh(x):
    c = 0.7978845608028654
    return 0.5 * x * (1.0 + jnp.tanh(c * (x + 0.044715 * (x * x * x))))


def _mixer_kernel(p_ref, x_ref, cw_ref, cb_ref, wg_ref, ba_ref, bx_ref, lam_ref,
                  wlo_ref, scw_ref, wso_ref, wo_ref, o_ref,
                  xbuf, cvbuf, a0, b0, a1, b1, hstate,
                  *, tm, pad, d_lru, d_sc, d_model, tiles_per_seq, n_groups, grp):
    i = pl.program_id(0)
    seq_start = (i % tiles_per_seq) == 0
    H = SUBLANES

    @pl.when(i == 0)
    def _():
        a0[0:pad, :] = jnp.ones((pad, d_lru), jnp.float32)
        a1[0:pad, :] = jnp.ones((pad, d_lru), jnp.float32)
        b0[0:pad, :] = jnp.zeros((pad, d_lru), jnp.float32)
        b1[0:pad, :] = jnp.zeros((pad, d_lru), jnp.float32)

    @pl.when(jnp.logical_not(seq_start))
    def _():
        xbuf[0:H, :] = xbuf[tm:tm + H, :]
        cvbuf[0:H, :] = cvbuf[tm:tm + H, :]

    @pl.when(seq_start)
    def _():
        xbuf[0:H, :] = jnp.zeros((H, d_lru), jnp.float32)
        cvbuf[0:H, :] = jnp.zeros((H, d_sc), jnp.float32)
        hstate[...] = jnp.zeros_like(hstate)

    o = 0
    lru_x = p_ref[:, o:o + d_lru]; o += d_lru
    lru_gate = p_ref[:, o:o + d_lru]; o += d_lru
    sc_b = p_ref[:, o:o + d_sc]; o += d_sc
    sc_c = p_ref[:, o:o + d_sc]; o += d_sc
    sc_v = p_ref[:, o:o + d_sc]; o += d_sc
    gate_lru = p_ref[:, o:o + d_model]; o += d_model
    gate_sc = p_ref[:, o:o + d_model]

    xbuf[H:H + tm, :] = lru_x
    cw = cw_ref[...]
    xc = (cw[3:4, :] * lru_x + cw[2:3, :] * xbuf[H - 1:H - 1 + tm, :]
          + cw[1:2, :] * xbuf[H - 2:H - 2 + tm, :] + cw[0:1, :] * xbuf[H - 3:H - 3 + tm, :]
          + cb_ref[...])
    xcb = xc.astype(jnp.bfloat16)
    lam_scale = -LRU_C * jax.nn.softplus(-lam_ref[...])
    for k in range(n_groups):
        sl = slice(k * grp, (k + 1) * grp)
        z = jnp.dot(xcb[:, sl], wg_ref[k], preferred_element_type=jnp.float32)
        r = jax.nn.sigmoid(z[:, :grp] + ba_ref[:, sl])
        ig = jax.nn.sigmoid(z[:, grp:] + bx_ref[:, sl])
        log_a = lam_scale[:, sl] * r
        a = jnp.exp(log_a)
        mult = jnp.sqrt(-jnp.tanh(log_a) * (a * a + 1.0))
        a0[pad:pad + tm, sl] = a
        b0[pad:pad + tm, sl] = mult * (ig * xc[:, sl])

    bufs = ((a0, b0), (a1, b1))
    cur = 0
    s = 1
    while s < tm:
        (ca, cb), (na, nb) = bufs[cur], bufs[1 - cur]
        a_hi = ca[pad:pad + tm, :]
        b_hi = cb[pad:pad + tm, :]
        a_lo = ca[pad - s:pad - s + tm, :]
        b_lo = cb[pad - s:pad - s + tm, :]
        na[pad:pad + tm, :] = a_hi * a_lo
        nb[pad:pad + tm, :] = a_hi * b_lo + b_hi
        cur = 1 - cur
        s *= 2
    fa, fb = bufs[cur]
    hseq = fa[pad:pad + tm, :] * hstate[0:1, :] + fb[pad:pad + tm, :]
    hstate[...] = jnp.broadcast_to(hseq[tm - 1:tm, :], (H, d_lru))

    y_in = (_gelu_tanh(lru_gate) * hseq).astype(jnp.bfloat16)
    y_lru = jnp.dot(y_in, wlo_ref[...], preferred_element_type=jnp.float32)

    cv = sc_c * sc_v
    cvbuf[H:H + tm, :] = cv
    scw = scw_ref[...]
    conv = (scw[2:3, :] * cv + scw[1:2, :] * cvbuf[H - 1:H - 1 + tm, :]
            + scw[0:1, :] * cvbuf[H - 2:H - 2 + tm, :])
    s_in = (sc_b * conv).astype(jnp.bfloat16)
    y_sc = jnp.dot(s_in, wso_ref[...], preferred_element_type=jnp.float32)

    merged = jax.nn.sigmoid(gate_lru) * y_lru + jax.nn.sigmoid(gate_sc) * y_sc
    o_ref[...] = x_ref[...] + jnp.dot(merged.astype(jnp.bfloat16), wo_ref[...],
                                      preferred_element_type=jnp.float32)


def _mixer(p, x2, cw, cb, wg, ba, bx, lam, wlo, scw, wso, wo, *, tm, seq):
    n, d_model = x2.shape
    cols = p.shape[1]
    d_lru = cw.shape[1]
    d_sc = scw.shape[1]
    n_groups, grp, _ = wg.shape
    pad = tm // 2
    const = lambda shape: pl.BlockSpec(shape, lambda i: (0,) * len(shape),
                                       pipeline_mode=pl.Buffered(1))
    kern = functools.partial(
        _mixer_kernel, tm=tm, pad=pad, d_lru=d_lru, d_sc=d_sc, d_model=d_model,
        tiles_per_seq=seq // tm, n_groups=n_groups, grp=grp)
    return pl.pallas_call(
        kern,
        grid=(n // tm,),
        in_specs=[
            pl.BlockSpec((tm, cols), lambda i: (i, 0)),
            pl.BlockSpec((tm, d_model), lambda i: (i, 0)),
            const(cw.shape), const(cb.shape), const(wg.shape), const(ba.shape),
            const(bx.shape), const(lam.shape), const(wlo.shape), const(scw.shape),
            const(wso.shape), const(wo.shape),
        ],
        out_specs=pl.BlockSpec((tm, d_model), lambda i: (i, 0)),
        out_shape=jax.ShapeDtypeStruct((n, d_model), jnp.float32),
        scratch_shapes=[
            pltpu.VMEM((tm + SUBLANES, d_lru), jnp.float32),
            pltpu.VMEM((tm + SUBLANES, d_sc), jnp.float32),
            pltpu.VMEM((pad + tm, d_lru), jnp.float32),
            pltpu.VMEM((pad + tm, d_lru), jnp.float32),
            pltpu.VMEM((pad + tm, d_lru), jnp.float32),
            pltpu.VMEM((pad + tm, d_lru), jnp.float32),
            pltpu.VMEM((SUBLANES, d_lru), jnp.float32),
        ],
        compiler_params=pltpu.CompilerParams(
            dimension_semantics=("arbitrary",),
            vmem_limit_bytes=VMEM_LIMIT_BYTES),
        name="mixer",
    )(p, x2, cw, cb, wg, ba, bx, lam, wlo, scw, wso, wo)


def _ffn_kernel(x_ref, g_ref, wg_ref, wv_ref, cwg_ref, cwv_ref, wd_ref, gf_ref, o_ref,
                h_ref, ubuf_g, ubuf_v, halo_g, halo_v, *, tm, tiles_per_seq, nj):
    i = pl.program_id(0)
    j = pl.program_id(1)
    H = SUBLANES
    seq_start = (i % tiles_per_seq) == 0

    @pl.when(j == 0)
    def _():
        x = x_ref[...]
        h_ref[...] = _rmsnorm(x, g_ref[...]).astype(jnp.bfloat16)
        o_ref[...] = x

    hb = h_ref[...]

    def conv_branch(w_ref, cw_ref, ubuf, halo):
        u = jnp.dot(hb, w_ref[...], preferred_element_type=jnp.float32)
        @pl.when(seq_start)
        def _():
            ubuf[0:H, :] = jnp.zeros((H, ubuf.shape[1]), jnp.float32)

        @pl.when(jnp.logical_not(seq_start))
        def _():
            ubuf[0:H, :] = halo[j]

        ubuf[H:H + tm, :] = u
        halo[j] = u[tm - H:tm, :]
        cw = cw_ref[...]
        return (cw[2:3, :] * u + cw[1:2, :] * ubuf[H - 1:H - 1 + tm, :]
                + cw[0:1, :] * ubuf[H - 2:H - 2 + tm, :])

    cg = conv_branch(wg_ref, cwg_ref, ubuf_g, halo_g)
    cv = conv_branch(wv_ref, cwv_ref, ubuf_v, halo_v)
    act = (cg * jax.nn.sigmoid(cg) * cv).astype(jnp.bfloat16)
    o_ref[...] += jnp.dot(act, wd_ref[...], preferred_element_type=jnp.float32)

    @pl.when(j == nj - 1)
    def _():
        o_ref[...] = _rmsnorm(o_ref[...], gf_ref[...])


def _conv_ffn(x1, g, w_up, conv_w, w_down, g_final, *, tm, tn, seq):
    n, d = x1.shape
    d_ff = w_down.shape[0]
    nj = d_ff // tn
    kern = functools.partial(_ffn_kernel, tm=tm, tiles_per_seq=seq // tm, nj=nj)
    return pl.pallas_call(
        kern,
        grid=(n // tm, nj),
        in_specs=[
            pl.BlockSpec((tm, d), lambda i, j: (i, 0)),
            pl.BlockSpec((1, d), lambda i, j: (0, 0)),
            pl.BlockSpec((d, tn), lambda i, j: (0, j)),
            pl.BlockSpec((d, tn), lambda i, j: (0, j + nj)),
            pl.BlockSpec((3, tn), lambda i, j: (0, j)),
            pl.BlockSpec((3, tn), lambda i, j: (0, j + nj)),
            pl.BlockSpec((tn, d), lambda i, j: (j, 0)),
            pl.BlockSpec((1, d), lambda i, j: (0, 0)),
        ],
        out_specs=pl.BlockSpec((tm, d), lambda i, j: (i, 0)),
        out_shape=jax.ShapeDtypeStruct((n, d), jnp.float32),
        scratch_shapes=[
            pltpu.VMEM((tm, d), jnp.bfloat16),
            pltpu.VMEM((tm + SUBLANES, tn), jnp.float32),
            pltpu.VMEM((tm + SUBLANES, tn), jnp.float32),
            pltpu.VMEM((nj, SUBLANES, tn), jnp.float32),
            pltpu.VMEM((nj, SUBLANES, tn), jnp.float32),
        ],
        compiler_params=pltpu.CompilerParams(
            dimension_semantics=("arbitrary", "arbitrary"),
            vmem_limit_bytes=VMEM_LIMIT_BYTES),
        name="conv_ffn",
    )(x1, g, w_up, w_up, conv_w, conv_w, w_down, g_final)


def _block_diag_gates(wa, wx, heads_per_group):
    n_heads, hd, _ = wa.shape
    n_groups = n_heads // heads_per_group
    eye = jnp.eye(heads_per_group, dtype=wa.dtype)

    def bd(w):
        w = w.reshape(n_groups, heads_per_group, hd, hd)
        full = jnp.einsum('ghij,hk->ghikj', w, eye)
        return full.reshape(n_groups, heads_per_group * hd, heads_per_group * hd)

    return jnp.concatenate([bd(wa), bd(wx)], axis=-1)


def kernel(x, g_mix, w_in, lru_conv_w, lru_conv_b, lru_wa, lru_ba, lru_wx, lru_bx, lru_lambda, lru_w_out, sc_conv_w, sc_w_out, w_o, g_ffn, ffn_w_up, ffn_conv_w, ffn_w_down, g_final):
    bsz, seq, d = x.shape
    n = bsz * seq
    bf = jnp.bfloat16
    row = lambda v: v.reshape(1, -1)
    x2 = x.reshape(n, d)

    p = _in_proj(x2, row(g_mix), w_in.astype(bf), tm=1024, tn=1024)

    mxu_width = 256
    heads_per_group = max(1, mxu_width // lru_wa.shape[1])
    wg = _block_diag_gates(lru_wa, lru_wx, heads_per_group).astype(bf)
    x1 = _mixer(p, x2, lru_conv_w, row(lru_conv_b), wg, row(lru_ba), row(lru_bx),
                row(lru_lambda), lru_w_out.astype(bf), sc_conv_w, sc_w_out.astype(bf),
                w_o.astype(bf), tm=256, seq=seq)

    out = _conv_ffn(x1, row(g_ffn), ffn_w_up.astype(bf), ffn_conv_w, ffn_w_down.astype(bf),
                    row(g_final), tm=512, tn=512, seq=seq)
    return out.reshape(bsz, seq, d)
```

```python
import functools

import jax
import jax.numpy as jnp
from jax import lax
from jax.experimental import pallas as pl
from jax.experimental.pallas import tpu as pltpu

EPS = 1e-6
LRU_C = 8.0
SUBLANES = 8
VMEM_LIMIT_BYTES = 56 * 1024 * 1024


def _rmsnorm(x, g):
    ms = jnp.mean(x * x, axis=-1, keepdims=True)
    return x * lax.rsqrt(ms + EPS) * g


def _in_proj_kernel(x_ref, g_ref, w_ref, o_ref, h_ref):
    @pl.when(pl.program_id(1) == 0)
    def _():
        h_ref[...] = _rmsnorm(x_ref[...], g_ref[...]).astype(jnp.bfloat16)

    o_ref[...] = jnp.dot(h_ref[...], w_ref[...], preferred_element_type=jnp.float32)


def _in_proj(x2, g, w_bf16, *, tm, tn):
    n, d = x2.shape
    cols = w_bf16.shape[1]
    return pl.pallas_call(
        _in_proj_kernel,
        grid=(n // tm, cols // tn),
        in_specs=[
            pl.BlockSpec((tm, d), lambda i, j: (i, 0)),
            pl.BlockSpec((1, d), lambda i, j: (0, 0)),
            pl.BlockSpec((d, tn), lambda i, j: (0, j)),
        ],
        out_specs=pl.BlockSpec((tm, tn), lambda i, j: (i, j)),
        out_shape=jax.ShapeDtypeStruct((n, cols), jnp.float32),
        scratch_shapes=[pltpu.VMEM((tm, d), jnp.bfloat16)],
        compiler_params=pltpu.CompilerParams(
            dimension_semantics=("arbitrary", "arbitrary"),
            vmem_limit_bytes=VMEM_LIMIT_BYTES),
        name="in_proj",
    )(x2, g, w_bf16)


def _gelu_tanh(x):
    c = 0.7978845608028654
    return 0.5 * x * (1.0 + jnp.tanh(c * (x + 0.044715 * (x * x * x))))


def _mixer_kernel(p_ref, x_ref, cw_ref, cb_ref, wg_ref, ba_ref, bx_ref, lam_ref,
                  wlo_ref, scw_ref, wso_ref, wo_ref, o_ref,
                  xbuf, cvbuf, a0, b0, a1, b1, hstate,
                  *, tm, pad, d_lru, d_sc, d_model, tiles_per_seq, n_groups, grp):
    i = pl.program_id(0)
    seq_start = (i % tiles_per_seq) == 0
    H = SUBLANES

    @pl.when(i == 0)
    def _():
        a0[0:pad, :] = jnp.ones((pad, d_lru), jnp.float32)
        a1[0:pad, :] = jnp.ones((pad, d_lru), jnp.float32)
        b0[0:pad, :] = jnp.zeros((pad, d_lru), jnp.float32)
        b1[0:pad, :] = jnp.zeros((pad, d_lru), jnp.float32)

    @pl.when(jnp.logical_not(seq_start))
    def _():
        xbuf[0:H, :] = xbuf[tm:tm + H, :]
        cvbuf[0:H, :] = cvbuf[tm:tm + H, :]

    @pl.when(seq_start)
    def _():
        xbuf[0:H, :] = jnp.zeros((H, d_lru), jnp.float32)
        cvbuf[0:H, :] = jnp.zeros((H, d_sc), jnp.float32)
        hstate[...] = jnp.zeros_like(hstate)

    o = 0
    lru_x = p_ref[:, o:o + d_lru]; o += d_lru
    lru_gate = p_ref[:, o:o + d_lru]; o += d_lru
    sc_b = p_ref[:, o:o + d_sc]; o += d_sc
    sc_c = p_ref[:, o:o + d_sc]; o += d_sc
    sc_v = p_ref[:, o:o + d_sc]; o += d_sc
    gate_lru = p_ref[:, o:o + d_model]; o += d_model
    gate_sc = p_ref[:, o:o + d_model]

    xbuf[H:H + tm, :] = lru_x
    cw = cw_ref[...]
    xc = (cw[3:4, :] * lru_x + cw[2:3, :] * xbuf[H - 1:H - 1 + tm, :]
          + cw[1:2, :] * xbuf[H - 2:H - 2 + tm, :] + cw[0:1, :] * xbuf[H - 3:H - 3 + tm, :]
          + cb_ref[...])
    xcb = xc.astype(jnp.bfloat16)
    lam_scale = -LRU_C * jax.nn.softplus(-lam_ref[...])
    for k in range(n_groups):
        sl = slice(k * grp, (k + 1) * grp)
        z = jnp.dot(xcb[:, sl], wg_ref[k], preferred_element_type=jnp.float32)
        r = jax.nn.sigmoid(z[:, :grp] + ba_ref[:, sl])
        ig = jax.nn.sigmoid(z[:, grp:] + bx_ref[:, sl])
        log_a = lam_scale[:, sl] * r
        a = jnp.exp(log_a)
        mult = jnp.sqrt(-jnp.tanh(log_a) * (a * a + 1.0))
        a0[pad:pad + tm, sl] = a
        b0[pad:pad + tm, sl] = mult * (ig * xc[:, sl])

    bufs = ((a0, b0), (a1, b1))
    cur = 0
    s = 1
    while s < tm:
        (ca, cb), (na, nb) = bufs[cur], bufs[1 - cur]
        a_hi = ca[pad:pad + tm, :]
        b_hi = cb[pad:pad + tm, :]
        a_lo = ca[pad - s:pad - s + tm, :]
        b_lo = cb[pad - s:pad - s + tm, :]
        na[pad:pad + tm, :] = a_hi * a_lo
        nb[pad:pad + tm, :] = a_hi * b_lo + b_hi
        cur = 1 - cur
        s *= 2
    fa, fb = bufs[cur]
    hseq = fa[pad:pad + tm, :] * hstate[0:1, :] + fb[pad:pad + tm, :]
    hstate[...] = jnp.broadcast_to(hseq[tm - 1:tm, :], (H, d_lru))

    y_in = (_gelu_tanh(lru_gate) * hseq).astype(jnp.bfloat16)
    y_lru = jnp.dot(y_in, wlo_ref[...], preferred_element_type=jnp.float32)

    cv = sc_c * sc_v
    cvbuf[H:H + tm, :] = cv
    scw = scw_ref[...]
    conv = (scw[2:3, :] * cv + scw[1:2, :] * cvbuf[H - 1:H - 1 + tm, :]
            + scw[0:1, :] * cvbuf[H - 2:H - 2 + tm, :])
    s_in = (sc_b * conv).astype(jnp.bfloat16)
    y_sc = jnp.dot(s_in, wso_ref[...], preferred_element_type=jnp.float32)

    merged = jax.nn.sigmoid(gate_lru) * y_lru + jax.nn.sigmoid(gate_sc) * y_sc
    o_ref[...] = x_ref[...] + jnp.dot(merged.astype(jnp.bfloat16), wo_ref[...],
                                      preferred_element_type=jnp.float32)


def _mixer(p, x2, cw, cb, wg, ba, bx, lam, wlo, scw, wso, wo, *, tm, seq):
    n, d_model = x2.shape
    cols = p.shape[1]
    d_lru = cw.shape[1]
    d_sc = scw.shape[1]
    n_groups, grp, _ = wg.shape
    pad = tm // 2
    const = lambda shape: pl.BlockSpec(shape, lambda i: (0,) * len(shape),
                                       pipeline_mode=pl.Buffered(1))
    kern = functools.partial(
        _mixer_kernel, tm=tm, pad=pad, d_lru=d_lru, d_sc=d_sc, d_model=d_model,
        tiles_per_seq=seq // tm, n_groups=n_groups, grp=grp)
    return pl.pallas_call(
        kern,
        grid=(n // tm,),
        in_specs=[
            pl.BlockSpec((tm, cols), lambda i: (i, 0)),
            pl.BlockSpec((tm, d_model), lambda i: (i, 0)),
            const(cw.shape), const(cb.shape), const(wg.shape), const(ba.shape),
            const(bx.shape), const(lam.shape), const(wlo.shape), const(scw.shape),
            const(wso.shape), const(wo.shape),
        ],
        out_specs=pl.BlockSpec((tm, d_model), lambda i: (i, 0)),
        out_shape=jax.ShapeDtypeStruct((n, d_model), jnp.float32),
        scratch_shapes=[
            pltpu.VMEM((tm + SUBLANES, d_lru), jnp.float32),
            pltpu.VMEM((tm + SUBLANES, d_sc), jnp.float32),
            pltpu.VMEM((pad + tm, d_lru), jnp.float32),
            pltpu.VMEM((pad + tm, d_lru), jnp.float32),
            pltpu.VMEM((pad + tm, d_lru), jnp.float32),
            pltpu.VMEM((pad + tm, d_lru), jnp.float32),
            pltpu.VMEM((SUBLANES, d_lru), jnp.float32),
        ],
        compiler_params=pltpu.CompilerParams(
            dimension_semantics=("arbitrary",),
            vmem_limit_bytes=VMEM_LIMIT_BYTES),
        name="mixer",
    )(p, x2, cw, cb, wg, ba, bx, lam, wlo, scw, wso, wo)


def _ffn_kernel(x_ref, g_ref, wg_ref, wv_ref, cwg_ref, cwv_ref, wd_ref, gf_ref, o_ref,
                h_ref, ubuf_g, ubuf_v, halo_g, halo_v, act_even, act_odd,
                *, tm, tiles_per_seq, nj, n_tiles):
    t = pl.program_id(0)
    H = SUBLANES
    t_up = jnp.minimum(t, n_tiles - 1)
    j_up = t_up % nj
    seq_start = ((t_up // nj) % tiles_per_seq) == 0
    j_down = jnp.maximum(t - 1, 0) % nj

    @pl.when(t == 0)
    def _():
        act_odd[...] = jnp.zeros_like(act_odd)

    @pl.when(jnp.logical_and(j_up == 0, t < n_tiles))
    def _():
        h_ref[...] = _rmsnorm(x_ref[...], g_ref[...]).astype(jnp.bfloat16)

    @pl.when(j_down == 0)
    def _():
        o_ref[...] = x_ref[...]

    @pl.when(seq_start)
    def _():
        ubuf_g[0:H, :] = jnp.zeros((H, ubuf_g.shape[1]), jnp.float32)
        ubuf_v[0:H, :] = jnp.zeros((H, ubuf_v.shape[1]), jnp.float32)

    @pl.when(jnp.logical_not(seq_start))
    def _():
        ubuf_g[0:H, :] = halo_g[j_up]
        ubuf_v[0:H, :] = halo_v[j_up]

    def conv_branch(u, cw_ref, ubuf, halo):
        ubuf[H:H + tm, :] = u
        halo[j_up] = u[tm - H:tm, :]
        cw = cw_ref[...]
        return (cw[2:3, :] * u + cw[1:2, :] * ubuf[H - 1:H - 1 + tm, :]
                + cw[0:1, :] * ubuf[H - 2:H - 2 + tm, :])

    def stages(act_prev, act_cur):
        hb = h_ref[...]
        ug = jnp.dot(hb, wg_ref[...], preferred_element_type=jnp.float32)
        uv = jnp.dot(hb, wv_ref[...], preferred_element_type=jnp.float32)
        o_ref[...] += jnp.dot(act_prev[...], wd_ref[...], preferred_element_type=jnp.float32)
        cg = conv_branch(ug, cwg_ref, ubuf_g, halo_g)
        cv = conv_branch(uv, cwv_ref, ubuf_v, halo_v)
        act_cur[...] = (cg * jax.nn.sigmoid(cg) * cv).astype(jnp.bfloat16)

    @pl.when(t % 2 == 0)
    def _():
        stages(act_odd, act_even)

    @pl.when(t % 2 == 1)
    def _():
        stages(act_even, act_odd)

    @pl.when(jnp.logical_and(j_down == nj - 1, t > 0))
    def _():
        o_ref[...] = _rmsnorm(o_ref[...], gf_ref[...])


def _conv_ffn(x1, g, w_up, conv_w, w_down, g_final, *, tm, tn, seq):
    n, d = x1.shape
    d_ff = w_down.shape[0]
    nj = d_ff // tn
    n_tiles = (n // tm) * nj
    kern = functools.partial(_ffn_kernel, tm=tm, tiles_per_seq=seq // tm, nj=nj,
                             n_tiles=n_tiles)
    up = lambda t: jnp.minimum(t, n_tiles - 1)
    down = lambda t: jnp.maximum(t - 1, 0)
    return pl.pallas_call(
        kern,
        grid=(n_tiles + 1,),
        in_specs=[
            pl.BlockSpec((tm, d), lambda t: (up(t) // nj, 0)),
            pl.BlockSpec((1, d), lambda t: (0, 0)),
            pl.BlockSpec((d, tn), lambda t: (0, up(t) % nj)),
            pl.BlockSpec((d, tn), lambda t: (0, up(t) % nj + nj)),
            pl.BlockSpec((3, tn), lambda t: (0, up(t) % nj)),
            pl.BlockSpec((3, tn), lambda t: (0, up(t) % nj + nj)),
            pl.BlockSpec((tn, d), lambda t: (down(t) % nj, 0)),
            pl.BlockSpec((1, d), lambda t: (0, 0)),
        ],
        out_specs=pl.BlockSpec((tm, d), lambda t: (down(t) // nj, 0)),
        out_shape=jax.ShapeDtypeStruct((n, d), jnp.float32),
        scratch_shapes=[
            pltpu.VMEM((tm, d), jnp.bfloat16),
            pltpu.VMEM((tm + SUBLANES, tn), jnp.float32),
            pltpu.VMEM((tm + SUBLANES, tn), jnp.float32),
            pltpu.VMEM((nj, SUBLANES, tn), jnp.float32),
            pltpu.VMEM((nj, SUBLANES, tn), jnp.float32),
            pltpu.VMEM((tm, tn), jnp.bfloat16),
            pltpu.VMEM((tm, tn), jnp.bfloat16),
        ],
        compiler_params=pltpu.CompilerParams(
            dimension_semantics=("arbitrary",),
            vmem_limit_bytes=VMEM_LIMIT_BYTES),
        name="conv_ffn",
    )(x1, g, w_up, w_up, conv_w, conv_w, w_down, g_final)


def _block_diag_gates(wa, wx, heads_per_group):
    n_heads, hd, _ = wa.shape
    n_groups = n_heads // heads_per_group
    eye = jnp.eye(heads_per_group, dtype=wa.dtype)

    def bd(w):
        w = w.reshape(n_groups, heads_per_group, hd, hd)
        full = jnp.einsum('ghij,hk->ghikj', w, eye)
        return full.reshape(n_groups, heads_per_group * hd, heads_per_group * hd)

    return jnp.concatenate([bd(wa), bd(wx)], axis=-1)


def kernel(x, g_mix, w_in, lru_conv_w, lru_conv_b, lru_wa, lru_ba, lru_wx, lru_bx, lru_lambda, lru_w_out, sc_conv_w, sc_w_out, w_o, g_ffn, ffn_w_up, ffn_conv_w, ffn_w_down, g_final):
    bsz, seq, d = x.shape
    n = bsz * seq
    bf = jnp.bfloat16
    row = lambda v: v.reshape(1, -1)
    x2 = x.reshape(n, d)

    p = _in_proj(x2, row(g_mix), w_in.astype(bf), tm=1024, tn=1024)

    mxu_width = 256
    heads_per_group = max(1, mxu_width // lru_wa.shape[1])
    wg = _block_diag_gates(lru_wa, lru_wx, heads_per_group).astype(bf)
    x1 = _mixer(p, x2, lru_conv_w, row(lru_conv_b), wg, row(lru_ba), row(lru_bx),
                row(lru_lambda), lru_w_out.astype(bf), sc_conv_w, sc_w_out.astype(bf),
                w_o.astype(bf), tm=256, seq=seq)

    out = _conv_ffn(x1, row(g_ffn), ffn_w_up.astype(bf), ffn_conv_w, ffn_w_down.astype(bf),
                    row(g_final), tm=512, tn=512, seq=seq)
    return out.reshape(bsz, seq, d)
```

```python
import functools

import jax
import jax.numpy as jnp
from jax import lax
from jax.experimental import pallas as pl
from jax.experimental.pallas import tpu as pltpu

EPS = 1e-6
LRU_C = 8.0
SUBLANES = 8
VMEM_LIMIT_BYTES = 56 * 1024 * 1024


def _rmsnorm(x, g):
    ms = jnp.mean(x * x, axis=-1, keepdims=True)
    return x * lax.rsqrt(ms + EPS) * g


def _gelu_tanh(x):
    c = 0.7978845608028654
    return 0.5 * x * (1.0 + jnp.tanh(c * (x + 0.044715 * (x * x * x))))


def _in_proj_kernel(x_ref, g_ref, w_ref, pa_ref, pg_ref, h_ref, *, n_main, gelu_col):
    j = pl.program_id(1)

    @pl.when(j == 0)
    def _():
        h_ref[...] = _rmsnorm(x_ref[...], g_ref[...]).astype(jnp.bfloat16)

    def tile():
        return jnp.dot(h_ref[...], w_ref[...], preferred_element_type=jnp.float32)

    @pl.when(jnp.logical_and(j < n_main, j != gelu_col))
    def _():
        pa_ref[...] = tile()

    @pl.when(j == gelu_col)
    def _():
        pa_ref[...] = _gelu_tanh(tile())

    @pl.when(j >= n_main)
    def _():
        pg_ref[...] = jax.nn.sigmoid(tile())


def _in_proj(x2, g, w_bf16, *, tm, tn, main_cols, gelu_col):
    n, d = x2.shape
    cols = w_bf16.shape[1]
    n_main = main_cols // tn
    kern = functools.partial(_in_proj_kernel, n_main=n_main, gelu_col=gelu_col)
    return pl.pallas_call(
        kern,
        grid=(n // tm, cols // tn),
        in_specs=[
            pl.BlockSpec((tm, d), lambda i, j: (i, 0)),
            pl.BlockSpec((1, d), lambda i, j: (0, 0)),
            pl.BlockSpec((d, tn), lambda i, j: (0, j)),
        ],
        out_specs=[
            pl.BlockSpec((tm, tn), lambda i, j: (i, jnp.minimum(j, n_main - 1))),
            pl.BlockSpec((tm, tn), lambda i, j: (i, jnp.maximum(j - n_main, 0))),
        ],
        out_shape=[
            jax.ShapeDtypeStruct((n, main_cols), jnp.float32),
            jax.ShapeDtypeStruct((n, cols - main_cols), jnp.float32),
        ],
        scratch_shapes=[pltpu.VMEM((tm, d), jnp.bfloat16)],
        compiler_params=pltpu.CompilerParams(
            dimension_semantics=("arbitrary", "arbitrary"),
            vmem_limit_bytes=VMEM_LIMIT_BYTES),
        name="in_proj",
    )(x2, g, w_bf16)


def _mixer_kernel(pa_ref, pg_ref, x_ref, cw_ref, cb_ref, wg_ref, ba_ref, bx_ref, lam_ref,
                  wlo_ref, scw_ref, wso_ref, wo_ref, o_ref,
                  xbuf, cvbuf, sa, sb, hstate, merged, y_even, s_even, y_odd, s_odd,
                  *, tm, pad, chunk, d_lru, d_sc, d_model, tiles_per_seq, n_groups, grp,
                  n_tiles):
    t = pl.program_id(0)
    H = SUBLANES
    seq_start = (jnp.minimum(t, n_tiles - 1) % tiles_per_seq) == 0

    @pl.when(t == 0)
    def _():
        sa[0:pad, :] = jnp.ones((pad, d_lru), jnp.float32)
        sb[0:pad, :] = jnp.zeros((pad, d_lru), jnp.float32)
        y_odd[...] = jnp.zeros_like(y_odd)
        s_odd[...] = jnp.zeros_like(s_odd)

    @pl.when(jnp.logical_not(seq_start))
    def _():
        xbuf[0:H, :] = xbuf[tm:tm + H, :]
        cvbuf[0:H, :] = cvbuf[tm:tm + H, :]

    @pl.when(seq_start)
    def _():
        xbuf[0:H, :] = jnp.zeros((H, d_lru), jnp.float32)
        cvbuf[0:H, :] = jnp.zeros((H, d_sc), jnp.float32)
        hstate[...] = jnp.zeros_like(hstate)

    n_blk = 2 * n_groups
    blk = d_model // n_blk

    def branch_proj(nb, y_prev, s_prev):
        cs = slice(nb * blk, (nb + 1) * blk)
        y_lru = jnp.dot(y_prev[...], wlo_ref[:, cs], preferred_element_type=jnp.float32)
        y_sc = jnp.dot(s_prev[...], wso_ref[:, cs], preferred_element_type=jnp.float32)
        gs = slice(d_model + nb * blk, d_model + (nb + 1) * blk)
        merged[:, cs] = (pg_ref[:, cs] * y_lru + pg_ref[:, gs] * y_sc).astype(jnp.bfloat16)

    def out_proj(nb):
        cs = slice(nb * blk, (nb + 1) * blk)
        o_ref[:, cs] = x_ref[:, cs] + jnp.dot(merged[...], wo_ref[:, cs],
                                              preferred_element_type=jnp.float32)

    def lru_group(k, y_cur):
        sl = slice(k * grp, (k + 1) * grp)
        lru_x = pa_ref[:, sl]
        lru_gate = pa_ref[:, d_lru + k * grp:d_lru + (k + 1) * grp]
        xbuf[H:H + tm, sl] = lru_x
        xc = (cw_ref[3:4, sl] * lru_x + cw_ref[2:3, sl] * xbuf[H - 1:H - 1 + tm, sl]
              + cw_ref[1:2, sl] * xbuf[H - 2:H - 2 + tm, sl]
              + cw_ref[0:1, sl] * xbuf[H - 3:H - 3 + tm, sl] + cb_ref[:, sl])
        z = jnp.dot(xc.astype(jnp.bfloat16), wg_ref[k], preferred_element_type=jnp.float32)
        r = jax.nn.sigmoid(z[:, :grp] + ba_ref[:, sl])
        ig = jax.nn.sigmoid(z[:, grp:] + bx_ref[:, sl])
        log_a = (-LRU_C * jax.nn.softplus(-lam_ref[:, sl])) * r
        a = jnp.exp(log_a)
        mult = jnp.sqrt(-jnp.tanh(log_a) * (a * a + 1.0))
        sa[pad:pad + tm, sl] = a
        sb[pad:pad + tm, sl] = mult * (ig * xc)

        s = 1
        while s < tm:
            for lo in range(tm - chunk, -1, -chunk):
                hi = pl.ds(pad + lo, chunk)
                sh = pl.ds(pad + lo - s, chunk)
                a_hi = sa[hi, sl]
                new_a = a_hi * sa[sh, sl]
                new_b = a_hi * sb[sh, sl] + sb[hi, sl]
                sa[hi, sl] = new_a
                sb[hi, sl] = new_b
            s *= 2
        hseq = sa[pad:pad + tm, sl] * hstate[0:1, sl] + sb[pad:pad + tm, sl]
        hstate[:, sl] = jnp.broadcast_to(hseq[tm - 1:tm, :], (H, grp))
        y_cur[:, sl] = (lru_gate * hseq).astype(jnp.bfloat16)

    def sc_group(k, s_cur):
        sl = slice(k * grp, (k + 1) * grp)
        o = 2 * d_lru + k * grp
        sc_b = pa_ref[:, o:o + grp]
        cv = pa_ref[:, o + d_sc:o + d_sc + grp] * pa_ref[:, o + 2 * d_sc:o + 2 * d_sc + grp]
        cvbuf[H:H + tm, sl] = cv
        conv = (scw_ref[2:3, sl] * cv + scw_ref[1:2, sl] * cvbuf[H - 1:H - 1 + tm, sl]
                + scw_ref[0:1, sl] * cvbuf[H - 2:H - 2 + tm, sl])
        s_cur[:, sl] = (sc_b * conv).astype(jnp.bfloat16)

    def step(y_prev, s_prev, y_cur, s_cur):
        for k in range(n_groups):
            branch_proj(2 * k, y_prev, s_prev)
            branch_proj(2 * k + 1, y_prev, s_prev)
            lru_group(k, y_cur)
        for k in range(n_groups):
            out_proj(2 * k)
            out_proj(2 * k + 1)
            sc_group(k, s_cur)

    @pl.when(t % 2 == 0)
    def _():
        step(y_odd, s_odd, y_even, s_even)

    @pl.when(t % 2 == 1)
    def _():
        step(y_even, s_even, y_odd, s_odd)


def _mixer(pa, pg, x2, cw, cb, wg, ba, bx, lam, wlo, scw, wso, wo, *, tm, seq):
    n, d_model = x2.shape
    d_lru = cw.shape[1]
    d_sc = scw.shape[1]
    n_groups, grp, _ = wg.shape
    n_tiles = n // tm
    pad = tm // 2
    chunk = min(tm, 64)
    const = lambda shape: pl.BlockSpec(shape, lambda t: (0,) * len(shape),
                                       pipeline_mode=pl.Buffered(1))
    seq_tile = lambda t: (jnp.minimum(t, n_tiles - 1), 0)
    proj_tile = lambda t: (jnp.maximum(t - 1, 0), 0)
    kern = functools.partial(
        _mixer_kernel, tm=tm, pad=pad, chunk=chunk, d_lru=d_lru, d_sc=d_sc, d_model=d_model,
        tiles_per_seq=seq // tm, n_groups=n_groups, grp=grp, n_tiles=n_tiles)
    handover = pltpu.VMEM((tm, d_lru), jnp.bfloat16)
    return pl.pallas_call(
        kern,
        grid=(n_tiles + 1,),
        in_specs=[
            pl.BlockSpec((tm, pa.shape[1]), seq_tile),
            pl.BlockSpec((tm, pg.shape[1]), proj_tile),
            pl.BlockSpec((tm, d_model), proj_tile),
            const(cw.shape), const(cb.shape), const(wg.shape), const(ba.shape),
            const(bx.shape), const(lam.shape), const(wlo.shape), const(scw.shape),
            const(wso.shape), const(wo.shape),
        ],
        out_specs=pl.BlockSpec((tm, d_model), proj_tile),
        out_shape=jax.ShapeDtypeStruct((n, d_model), jnp.float32),
        scratch_shapes=[
            pltpu.VMEM((tm + SUBLANES, d_lru), jnp.float32),
            pltpu.VMEM((tm + SUBLANES, d_sc), jnp.float32),
            pltpu.VMEM((pad + tm, d_lru), jnp.float32),
            pltpu.VMEM((pad + tm, d_lru), jnp.float32),
            pltpu.VMEM((SUBLANES, d_lru), jnp.float32),
            pltpu.VMEM((tm, d_model), jnp.bfloat16),
            handover, handover, handover, handover,
        ],
        compiler_params=pltpu.CompilerParams(
            dimension_semantics=("arbitrary",),
            vmem_limit_bytes=VMEM_LIMIT_BYTES),
        name="mixer",
    )(pa, pg, x2, cw, cb, wg, ba, bx, lam, wlo, scw, wso, wo)


def _ffn_kernel(x_ref, g_ref, wg_ref, wv_ref, cwg_ref, cwv_ref, wd_ref, gf_ref, o_ref,
                h_ref, ubuf_g, ubuf_v, halo_g, halo_v, act_even, act_odd,
                *, tm, tiles_per_seq, nj, n_tiles):
    t = pl.program_id(0)
    H = SUBLANES
    t_up = jnp.minimum(t, n_tiles - 1)
    j_up = t_up % nj
    seq_start = ((t_up // nj) % tiles_per_seq) == 0
    j_down = jnp.maximum(t - 1, 0) % nj

    @pl.when(t == 0)
    def _():
        act_odd[...] = jnp.zeros_like(act_odd)

    @pl.when(jnp.logical_and(j_up == 0, t < n_tiles))
    def _():
        h_ref[...] = _rmsnorm(x_ref[...], g_ref[...]).astype(jnp.bfloat16)

    @pl.when(j_down == 0)
    def _():
        o_ref[...] = x_ref[...]

    @pl.when(seq_start)
    def _():
        ubuf_g[0:H, :] = jnp.zeros((H, ubuf_g.shape[1]), jnp.float32)
        ubuf_v[0:H, :] = jnp.zeros((H, ubuf_v.shape[1]), jnp.float32)

    @pl.when(jnp.logical_not(seq_start))
    def _():
        ubuf_g[0:H, :] = halo_g[j_up]
        ubuf_v[0:H, :] = halo_v[j_up]

    def conv_branch(u, cw_ref, ubuf, halo):
        ubuf[H:H + tm, :] = u
        halo[j_up] = u[tm - H:tm, :]
        cw = cw_ref[...]
        return (cw[2:3, :] * u + cw[1:2, :] * ubuf[H - 1:H - 1 + tm, :]
                + cw[0:1, :] * ubuf[H - 2:H - 2 + tm, :])

    def stages(act_prev, act_cur):
        hb = h_ref[...]
        ug = jnp.dot(hb, wg_ref[...], preferred_element_type=jnp.float32)
        uv = jnp.dot(hb, wv_ref[...], preferred_element_type=jnp.float32)
        o_ref[...] += jnp.dot(act_prev[...], wd_ref[...], preferred_element_type=jnp.float32)
        cg = conv_branch(ug, cwg_ref, ubuf_g, halo_g)
        cv = conv_branch(uv, cwv_ref, ubuf_v, halo_v)
        act_cur[...] = (cg * jax.nn.sigmoid(cg) * cv).astype(jnp.bfloat16)

    @pl.when(t % 2 == 0)
    def _():
        stages(act_odd, act_even)

    @pl.when(t % 2 == 1)
    def _():
        stages(act_even, act_odd)

    @pl.when(jnp.logical_and(j_down == nj - 1, t > 0))
    def _():
        o_ref[...] = _rmsnorm(o_ref[...], gf_ref[...])


def _conv_ffn(x1, g, w_up, conv_w, w_down, g_final, *, tm, tn, seq):
    n, d = x1.shape
    d_ff = w_down.shape[0]
    nj = d_ff // tn
    n_tiles = (n // tm) * nj
    kern = functools.partial(_ffn_kernel, tm=tm, tiles_per_seq=seq // tm, nj=nj,
                             n_tiles=n_tiles)
    up = lambda t: jnp.minimum(t, n_tiles - 1)
    down = lambda t: jnp.maximum(t - 1, 0)
    return pl.pallas_call(
        kern,
        grid=(n_tiles + 1,),
        in_specs=[
            pl.BlockSpec((tm, d), lambda t: (up(t) // nj, 0)),
            pl.BlockSpec((1, d), lambda t: (0, 0)),
            pl.BlockSpec((d, tn), lambda t: (0, up(t) % nj)),
            pl.BlockSpec((d, tn), lambda t: (0, up(t) % nj + nj)),
            pl.BlockSpec((3, tn), lambda t: (0, up(t) % nj)),
            pl.BlockSpec((3, tn), lambda t: (0, up(t) % nj + nj)),
            pl.BlockSpec((tn, d), lambda t: (down(t) % nj, 0)),
            pl.BlockSpec((1, d), lambda t: (0, 0)),
        ],
        out_specs=pl.BlockSpec((tm, d), lambda t: (down(t) // nj, 0)),
        out_shape=jax.ShapeDtypeStruct((n, d), jnp.float32),
        scratch_shapes=[
            pltpu.VMEM((tm, d), jnp.bfloat16),
            pltpu.VMEM((tm + SUBLANES, tn), jnp.float32),
            pltpu.VMEM((tm + SUBLANES, tn), jnp.float32),
            pltpu.VMEM((nj, SUBLANES, tn), jnp.float32),
            pltpu.VMEM((nj, SUBLANES, tn), jnp.float32),
            pltpu.VMEM((tm, tn), jnp.bfloat16),
            pltpu.VMEM((tm, tn), jnp.bfloat16),
        ],
        compiler_params=pltpu.CompilerParams(
            dimension_semantics=("arbitrary",),
            vmem_limit_bytes=VMEM_LIMIT_BYTES),
        name="conv_ffn",
    )(x1, g, w_up, w_up, conv_w, conv_w, w_down, g_final)


def _block_diag_gates(wa, wx, heads_per_group):
    n_heads, hd, _ = wa.shape
    n_groups = n_heads // heads_per_group
    eye = jnp.eye(heads_per_group, dtype=wa.dtype)

    def bd(w):
        w = w.reshape(n_groups, heads_per_group, hd, hd)
        full = jnp.einsum('ghij,hk->ghikj', w, eye)
        return full.reshape(n_groups, heads_per_group * hd, heads_per_group * hd)

    return jnp.concatenate([bd(wa), bd(wx)], axis=-1)


def kernel(x, g_mix, w_in, lru_conv_w, lru_conv_b, lru_wa, lru_ba, lru_wx, lru_bx, lru_lambda, lru_w_out, sc_conv_w, sc_w_out, w_o, g_ffn, ffn_w_up, ffn_conv_w, ffn_w_down, g_final):
    bsz, seq, d = x.shape
    n = bsz * seq
    bf = jnp.bfloat16
    row = lambda v: v.reshape(1, -1)
    x2 = x.reshape(n, d)
    d_lru = lru_conv_w.shape[1]
    d_sc = sc_conv_w.shape[1]

    tn_in = 1024
    main_cols = 2 * d_lru + 3 * d_sc
    assert d_lru == tn_in and main_cols % tn_in == 0 and (w_in.shape[1] - main_cols) == 2 * d
    pa, pg = _in_proj(x2, row(g_mix), w_in.astype(bf), tm=1024, tn=tn_in,
                      main_cols=main_cols, gelu_col=1)

    mxu_width = 256
    heads_per_group = max(1, mxu_width // lru_wa.shape[1])
    wg = _block_diag_gates(lru_wa, lru_wx, heads_per_group).astype(bf)
    x1 = _mixer(pa, pg, x2, lru_conv_w, row(lru_conv_b), wg, row(lru_ba), row(lru_bx),
                row(lru_lambda), lru_w_out.astype(bf), sc_conv_w, sc_w_out.astype(bf),
                w_o.astype(bf), tm=256, seq=seq)

    out = _conv_ffn(x1, row(g_ffn), ffn_w_up.astype(bf), ffn_conv_w, ffn_w_down.astype(bf),
                    row(g_final), tm=512, tn=512, seq=seq)
    return out.reshape(bsz, seq, d)
```

```python
import functools

import jax
import jax.numpy as jnp
from jax import lax
from jax.experimental import pallas as pl
from jax.experimental.pallas import tpu as pltpu

EPS = 1e-6
LRU_C = 8.0
SUBLANES = 8
BF16_SUBLANES = 16
VMEM_LIMIT_BYTES = 56 * 1024 * 1024


def _rmsnorm(x, g):
    ms = jnp.mean(x * x, axis=-1, keepdims=True)
    return x * lax.rsqrt(ms + EPS) * g


def _sigmoid(x):
    return 0.5 * jnp.tanh(0.5 * x) + 0.5


def _gelu_tanh(x):
    c = 0.7978845608028654
    return 0.5 * x * (1.0 + jnp.tanh(c * (x + 0.044715 * (x * x * x))))


def _in_proj_kernel(x_ref, g_ref, w_ref, *refs, n_side):
    side_in = refs[:n_side]
    o_ref = refs[n_side]
    side_out = refs[n_side + 1:2 * n_side + 1]
    h_ref = refs[2 * n_side + 1]

    @pl.when(pl.program_id(1) == 0)
    def _():
        h_ref[...] = _rmsnorm(x_ref[...], g_ref[...]).astype(jnp.bfloat16)

    o_ref[...] = jnp.dot(h_ref[...], w_ref[...], preferred_element_type=jnp.float32)
    for src, dst in zip(side_in, side_out):
        dst[...] = src[...].astype(jnp.bfloat16)


def _in_proj(x2, g, w_bf16, side_weights, *, tm, tn):
    n, d = x2.shape
    cols = w_bf16.shape[1]
    ni, nj = n // tm, cols // tn
    side_specs, side_shapes = [], []
    for w in side_weights:
        rows = w.shape[0]
        per_row_tile = max(c for c in range(1, nj + 1)
                           if rows % (ni * c) == 0 and (rows // (ni * c)) % BF16_SUBLANES == 0)
        blk = rows // (ni * per_row_tile)
        stride = nj // per_row_tile
        imap = functools.partial(
            lambda i, j, c, s: (i * c + jnp.minimum(j // s, c - 1), 0), c=per_row_tile, s=stride)
        side_specs.append(pl.BlockSpec((blk, w.shape[1]), imap))
        side_shapes.append(jax.ShapeDtypeStruct(w.shape, jnp.bfloat16))
    outs = pl.pallas_call(
        functools.partial(_in_proj_kernel, n_side=len(side_weights)),
        grid=(ni, nj),
        in_specs=[
            pl.BlockSpec((tm, d), lambda i, j: (i, 0)),
            pl.BlockSpec((1, d), lambda i, j: (0, 0)),
            pl.BlockSpec((d, tn), lambda i, j: (0, j)),
        ] + side_specs,
        out_specs=[pl.BlockSpec((tm, tn), lambda i, j: (i, j))] + side_specs,
        out_shape=[jax.ShapeDtypeStruct((n, cols), jnp.float32)] + side_shapes,
        scratch_shapes=[pltpu.VMEM((tm, d), jnp.bfloat16)],
        compiler_params=pltpu.CompilerParams(
            dimension_semantics=("arbitrary", "arbitrary"),
            vmem_limit_bytes=VMEM_LIMIT_BYTES),
        name="in_proj",
    )(x2, g, w_bf16, *side_weights)
    return outs[0], outs[1:]


def _mixer_kernel(p_ref, x_ref, cw_ref, cb_ref, wg_ref, ba_ref, bx_ref, lam_ref,
                  wlo_ref, scw_ref, wso_ref, wo_ref, o_ref,
                  xbuf, cvbuf, a0, b0, a1, b1, hstate,
                  *, tm, pad, d_lru, d_sc, d_model, tiles_per_seq, n_groups, grp):
    i = pl.program_id(0)
    seq_start = (i % tiles_per_seq) == 0
    H = SUBLANES

    @pl.when(i == 0)
    def _():
        a0[0:pad, :] = jnp.ones((pad, d_lru), jnp.float32)
        a1[0:pad, :] = jnp.ones((pad, d_lru), jnp.float32)
        b0[0:pad, :] = jnp.zeros((pad, d_lru), jnp.float32)
        b1[0:pad, :] = jnp.zeros((pad, d_lru), jnp.float32)

    @pl.when(jnp.logical_not(seq_start))
    def _():
        xbuf[0:H, :] = xbuf[tm:tm + H, :]
        cvbuf[0:H, :] = cvbuf[tm:tm + H, :]

    @pl.when(seq_start)
    def _():
        xbuf[0:H, :] = jnp.zeros((H, d_lru), jnp.float32)
        cvbuf[0:H, :] = jnp.zeros((H, d_sc), jnp.float32)
        hstate[...] = jnp.zeros_like(hstate)

    o = 0
    lru_x = p_ref[:, o:o + d_lru]; o += d_lru
    lru_gate = p_ref[:, o:o + d_lru]; o += d_lru
    sc_b = p_ref[:, o:o + d_sc]; o += d_sc
    sc_c = p_ref[:, o:o + d_sc]; o += d_sc
    sc_v = p_ref[:, o:o + d_sc]; o += d_sc
    gate_lru = p_ref[:, o:o + d_model]; o += d_model
    gate_sc = p_ref[:, o:o + d_model]

    cv = sc_c * sc_v
    cvbuf[H:H + tm, :] = cv
    scw = scw_ref[...]
    conv = (scw[2:3, :] * cv + scw[1:2, :] * cvbuf[H - 1:H - 1 + tm, :]
            + scw[0:1, :] * cvbuf[H - 2:H - 2 + tm, :])
    s_in = (sc_b * conv).astype(jnp.bfloat16)
    y_sc = jnp.dot(s_in, wso_ref[...], preferred_element_type=jnp.float32)

    xbuf[H:H + tm, :] = lru_x
    cw = cw_ref[...]
    xc = (cw[3:4, :] * lru_x + cw[2:3, :] * xbuf[H - 1:H - 1 + tm, :]
          + cw[1:2, :] * xbuf[H - 2:H - 2 + tm, :] + cw[0:1, :] * xbuf[H - 3:H - 3 + tm, :]
          + cb_ref[...])
    xcb = xc.astype(jnp.bfloat16)
    lam_scale = -LRU_C * jax.nn.softplus(-lam_ref[...])
    for k in range(n_groups):
        sl = slice(k * grp, (k + 1) * grp)
        z = jnp.dot(xcb[:, sl], wg_ref[k], preferred_element_type=jnp.float32)
        r = _sigmoid(z[:, :grp] + ba_ref[:, sl])
        ig = _sigmoid(z[:, grp:] + bx_ref[:, sl])
        log_a = lam_scale[:, sl] * r
        a = jnp.exp(log_a)
        mult = jnp.sqrt(-jnp.tanh(log_a) * (a * a + 1.0))
        a0[pad:pad + tm, sl] = a
        b0[pad:pad + tm, sl] = mult * (ig * xc[:, sl])

    bufs = ((a0, b0), (a1, b1))
    cur = 0
    s = 1
    while s < tm:
        (ca, cb), (na, nb) = bufs[cur], bufs[1 - cur]
        a_hi = ca[pad:pad + tm, :]
        b_hi = cb[pad:pad + tm, :]
        a_lo = ca[pad - s:pad - s + tm, :]
        b_lo = cb[pad - s:pad - s + tm, :]
        na[pad:pad + tm, :] = a_hi * a_lo
        nb[pad:pad + tm, :] = a_hi * b_lo + b_hi
        cur = 1 - cur
        s *= 2
    fa, fb = bufs[cur]
    hseq = fa[pad:pad + tm, :] * hstate[0:1, :] + fb[pad:pad + tm, :]
    hstate[...] = jnp.broadcast_to(hseq[tm - 1:tm, :], (H, d_lru))

    y_in = (_gelu_tanh(lru_gate) * hseq).astype(jnp.bfloat16)
    y_lru = jnp.dot(y_in, wlo_ref[...], preferred_element_type=jnp.float32)

    merged = _sigmoid(gate_lru) * y_lru + _sigmoid(gate_sc) * y_sc
    o_ref[...] = x_ref[...] + jnp.dot(merged.astype(jnp.bfloat16), wo_ref[...],
                                      preferred_element_type=jnp.float32)


def _mixer(p, x2, cw, cb, wg, ba, bx, lam, wlo, scw, wso, wo, *, tm, seq):
    n, d_model = x2.shape
    cols = p.shape[1]
    d_lru = cw.shape[1]
    d_sc = scw.shape[1]
    n_groups, grp, _ = wg.shape
    pad = tm // 2
    const = lambda shape: pl.BlockSpec(shape, lambda i: (0,) * len(shape),
                                       pipeline_mode=pl.Buffered(1))
    kern = functools.partial(
        _mixer_kernel, tm=tm, pad=pad, d_lru=d_lru, d_sc=d_sc, d_model=d_model,
        tiles_per_seq=seq // tm, n_groups=n_groups, grp=grp)
    return pl.pallas_call(
        kern,
        grid=(n // tm,),
        in_specs=[
            pl.BlockSpec((tm, cols), lambda i: (i, 0)),
            pl.BlockSpec((tm, d_model), lambda i: (i, 0)),
            const(cw.shape), const(cb.shape), const(wg.shape), const(ba.shape),
            const(bx.shape), const(lam.shape), const(wlo.shape), const(scw.shape),
            const(wso.shape), const(wo.shape),
        ],
        out_specs=pl.BlockSpec((tm, d_model), lambda i: (i, 0)),
        out_shape=jax.ShapeDtypeStruct((n, d_model), jnp.float32),
        scratch_shapes=[
            pltpu.VMEM((tm + SUBLANES, d_lru), jnp.float32),
            pltpu.VMEM((tm + SUBLANES, d_sc), jnp.float32),
            pltpu.VMEM((pad + tm, d_lru), jnp.float32),
            pltpu.VMEM((pad + tm, d_lru), jnp.float32),
            pltpu.VMEM((pad + tm, d_lru), jnp.float32),
            pltpu.VMEM((pad + tm, d_lru), jnp.float32),
            pltpu.VMEM((SUBLANES, d_lru), jnp.float32),
        ],
        compiler_params=pltpu.CompilerParams(
            dimension_semantics=("arbitrary",),
            vmem_limit_bytes=VMEM_LIMIT_BYTES),
        name="mixer",
    )(p, x2, cw, cb, wg, ba, bx, lam, wlo, scw, wso, wo)


def _ffn_kernel(x_ref, g_ref, wg_ref, wv_ref, cwg_ref, cwv_ref, wd_ref, gf_ref, o_ref,
                h_ref, ubuf_g, ubuf_v, halo_g, halo_v, act_even, act_odd,
                *, tm, tiles_per_seq, nj, n_tiles):
    t = pl.program_id(0)
    H = SUBLANES
    t_up = jnp.minimum(t, n_tiles - 1)
    j_up = t_up % nj
    seq_start = ((t_up // nj) % tiles_per_seq) == 0
    j_down = jnp.maximum(t - 1, 0) % nj

    @pl.when(t == 0)
    def _():
        act_odd[...] = jnp.zeros_like(act_odd)

    @pl.when(jnp.logical_and(j_up == 0, t < n_tiles))
    def _():
        h_ref[...] = _rmsnorm(x_ref[...], g_ref[...]).astype(jnp.bfloat16)

    @pl.when(j_down == 0)
    def _():
        o_ref[...] = x_ref[...]

    @pl.when(seq_start)
    def _():
        ubuf_g[0:H, :] = jnp.zeros((H, ubuf_g.shape[1]), jnp.float32)
        ubuf_v[0:H, :] = jnp.zeros((H, ubuf_v.shape[1]), jnp.float32)

    @pl.when(jnp.logical_not(seq_start))
    def _():
        ubuf_g[0:H, :] = halo_g[j_up]
        ubuf_v[0:H, :] = halo_v[j_up]

    def conv_branch(u, cw_ref, ubuf, halo):
        ubuf[H:H + tm, :] = u
        halo[j_up] = u[tm - H:tm, :]
        cw = cw_ref[...]
        return (cw[2:3, :] * u + cw[1:2, :] * ubuf[H - 1:H - 1 + tm, :]
                + cw[0:1, :] * ubuf[H - 2:H - 2 + tm, :])

    def stages(act_prev, act_cur):
        hb = h_ref[...]
        ug = jnp.dot(hb, wg_ref[...], preferred_element_type=jnp.float32)
        uv = jnp.dot(hb, wv_ref[...], preferred_element_type=jnp.float32)
        o_ref[...] += jnp.dot(act_prev[...], wd_ref[...], preferred_element_type=jnp.float32)
        cg = conv_branch(ug, cwg_ref, ubuf_g, halo_g)
        cv = conv_branch(uv, cwv_ref, ubuf_v, halo_v)
        act_cur[...] = (cg * _sigmoid(cg) * cv).astype(jnp.bfloat16)

    @pl.when(t % 2 == 0)
    def _():
        stages(act_odd, act_even)

    @pl.when(t % 2 == 1)
    def _():
        stages(act_even, act_odd)

    @pl.when(jnp.logical_and(j_down == nj - 1, t > 0))
    def _():
        o_ref[...] = _rmsnorm(o_ref[...], gf_ref[...])


def _conv_ffn(x1, g, w_up, conv_w, w_down, g_final, *, tm, tn, seq):
    n, d = x1.shape
    d_ff = w_down.shape[0]
    nj = d_ff // tn
    n_tiles = (n // tm) * nj
    kern = functools.partial(_ffn_kernel, tm=tm, tiles_per_seq=seq // tm, nj=nj,
                             n_tiles=n_tiles)
    up = lambda t: jnp.minimum(t, n_tiles - 1)
    down = lambda t: jnp.maximum(t - 1, 0)
    return pl.pallas_call(
        kern,
        grid=(n_tiles + 1,),
        in_specs=[
            pl.BlockSpec((tm, d), lambda t: (up(t) // nj, 0)),
            pl.BlockSpec((1, d), lambda t: (0, 0)),
            pl.BlockSpec((d, tn), lambda t: (0, up(t) % nj)),
            pl.BlockSpec((d, tn), lambda t: (0, up(t) % nj + nj)),
            pl.BlockSpec((3, tn), lambda t: (0, up(t) % nj)),
            pl.BlockSpec((3, tn), lambda t: (0, up(t) % nj + nj)),
            pl.BlockSpec((tn, d), lambda t: (down(t) % nj, 0)),
            pl.BlockSpec((1, d), lambda t: (0, 0)),
        ],
        out_specs=pl.BlockSpec((tm, d), lambda t: (down(t) // nj, 0)),
        out_shape=jax.ShapeDtypeStruct((n, d), jnp.float32),
        scratch_shapes=[
            pltpu.VMEM((tm, d), jnp.bfloat16),
            pltpu.VMEM((tm + SUBLANES, tn), jnp.float32),
            pltpu.VMEM((tm + SUBLANES, tn), jnp.float32),
            pltpu.VMEM((nj, SUBLANES, tn), jnp.float32),
            pltpu.VMEM((nj, SUBLANES, tn), jnp.float32),
            pltpu.VMEM((tm, tn), jnp.bfloat16),
            pltpu.VMEM((tm, tn), jnp.bfloat16),
        ],
        compiler_params=pltpu.CompilerParams(
            dimension_semantics=("arbitrary",),
            vmem_limit_bytes=VMEM_LIMIT_BYTES),
        name="conv_ffn",
    )(x1, g, w_up, w_up, conv_w, conv_w, w_down, g_final)


def _block_diag_gates(wa, wx, heads_per_group):
    n_heads, hd, _ = wa.shape
    n_groups = n_heads // heads_per_group
    eye = jnp.eye(heads_per_group, dtype=wa.dtype)

    def bd(w):
        w = w.reshape(n_groups, heads_per_group, hd, hd)
        full = jnp.einsum('ghij,hk->ghikj', w, eye)
        return full.reshape(n_groups, heads_per_group * hd, heads_per_group * hd)

    return jnp.concatenate([bd(wa), bd(wx)], axis=-1)


def kernel(x, g_mix, w_in, lru_conv_w, lru_conv_b, lru_wa, lru_ba, lru_wx, lru_bx, lru_lambda, lru_w_out, sc_conv_w, sc_w_out, w_o, g_ffn, ffn_w_up, ffn_conv_w, ffn_w_down, g_final):
    bsz, seq, d = x.shape
    n = bsz * seq
    bf = jnp.bfloat16
    row = lambda v: v.reshape(1, -1)
    x2 = x.reshape(n, d)

    p, (wlo, wso, wo, w_up, w_down) = _in_proj(
        x2, row(g_mix), w_in.astype(bf), [lru_w_out, sc_w_out, w_o, ffn_w_up, ffn_w_down],
        tm=1024, tn=1024)

    mxu_width = 256
    heads_per_group = max(1, mxu_width // lru_wa.shape[1])
    wg = _block_diag_gates(lru_wa, lru_wx, heads_per_group).astype(bf)
    x1 = _mixer(p, x2, lru_conv_w, row(lru_conv_b), wg, row(lru_ba), row(lru_bx),
                row(lru_lambda), wlo, sc_conv_w, wso, wo, tm=256, seq=seq)

    out = _conv_ffn(x1, row(g_ffn), w_up, ffn_conv_w, w_down, row(g_final),
                    tm=512, tn=512, seq=seq)
    return out.reshape(bsz, seq, d)
```

```python
import functools

import jax
import jax.numpy as jnp
from jax import lax
from jax.experimental import pallas as pl
from jax.experimental.pallas import tpu as pltpu

EPS = 1e-6
LRU_C = 8.0
SUBLANES = 8
BF16_SUBLANES = 16
VMEM_LIMIT_BYTES = 60 * 1024 * 1024


def _rmsnorm(x, g):
    ms = jnp.mean(x * x, axis=-1, keepdims=True)
    return x * lax.rsqrt(ms + EPS) * g


def _sigmoid(x):
    return 0.5 * jnp.tanh(0.5 * x) + 0.5


def _gelu_tanh(x):
    c = 0.7978845608028654
    return 0.5 * x * (1.0 + jnp.tanh(c * (x + 0.044715 * (x * x * x))))


def _in_proj_kernel(x_ref, g_ref, w_ref, *refs, n_side):
    side_in = refs[:n_side]
    o_ref = refs[n_side]
    side_out = refs[n_side + 1:2 * n_side + 1]
    h_ref = refs[2 * n_side + 1]

    @pl.when(pl.program_id(1) == 0)
    def _():
        h_ref[...] = _rmsnorm(x_ref[...], g_ref[...]).astype(jnp.bfloat16)

    o_ref[...] = jnp.dot(h_ref[...], w_ref[...],
                         preferred_element_type=jnp.float32).astype(o_ref.dtype)
    for src, dst in zip(side_in, side_out):
        dst[...] = src[...].astype(jnp.bfloat16)


def _in_proj(x2, g, w_bf16, side_weights, *, tm, tn):
    n, d = x2.shape
    cols = w_bf16.shape[1]
    ni, nj = n // tm, cols // tn
    side_specs, side_shapes = [], []
    for w in side_weights:
        rows = w.shape[0]
        per_row_tile = max(c for c in range(1, nj + 1)
                           if rows % (ni * c) == 0 and (rows // (ni * c)) % BF16_SUBLANES == 0)
        blk = rows // (ni * per_row_tile)
        stride = nj // per_row_tile
        imap = functools.partial(
            lambda i, j, c, s: (i * c + jnp.minimum(j // s, c - 1), 0), c=per_row_tile, s=stride)
        side_specs.append(pl.BlockSpec((blk, w.shape[1]), imap))
        side_shapes.append(jax.ShapeDtypeStruct(w.shape, jnp.bfloat16))
    outs = pl.pallas_call(
        functools.partial(_in_proj_kernel, n_side=len(side_weights)),
        grid=(ni, nj),
        in_specs=[
            pl.BlockSpec((tm, d), lambda i, j: (i, 0)),
            pl.BlockSpec((1, d), lambda i, j: (0, 0)),
            pl.BlockSpec((d, tn), lambda i, j: (0, j)),
        ] + side_specs,
        out_specs=[pl.BlockSpec((tm, tn), lambda i, j: (i, j))] + side_specs,
        out_shape=[jax.ShapeDtypeStruct((n, cols), jnp.bfloat16)] + side_shapes,
        scratch_shapes=[pltpu.VMEM((tm, d), jnp.bfloat16)],
        compiler_params=pltpu.CompilerParams(
            dimension_semantics=("arbitrary", "arbitrary"),
            vmem_limit_bytes=VMEM_LIMIT_BYTES),
        name="in_proj",
    )(x2, g, w_bf16, *side_weights)
    return outs[0], outs[1:]


def _mixer_kernel(p_ref, x_ref, cw_ref, cb_ref, wg_ref, ba_ref, bx_ref, lam_ref,
                  wlo_ref, scw_ref, wso_ref, wo_ref, o_ref,
                  xbuf, cvbuf, sa, sb, hstate, acc_l, acc_s,
                  *, tm, pad, chunk, d_lru, d_sc, d_model, tiles_per_seq, n_groups, grp):
    i = pl.program_id(0)
    seq_start = (i % tiles_per_seq) == 0
    H = SUBLANES
    f32 = jnp.float32
    bf16 = jnp.bfloat16

    @pl.when(i == 0)
    def _():
        sa[0:pad, :] = jnp.ones((pad, d_lru), f32)
        sb[0:pad, :] = jnp.zeros((pad, d_lru), f32)

    @pl.when(jnp.logical_not(seq_start))
    def _():
        xbuf[0:H, :] = xbuf[tm:tm + H, :]
        cvbuf[0:H, :] = cvbuf[tm:tm + H, :]

    @pl.when(seq_start)
    def _():
        xbuf[0:H, :] = jnp.zeros((H, d_lru), f32)
        cvbuf[0:H, :] = jnp.zeros((H, d_sc), f32)
        hstate[...] = jnp.zeros_like(hstate)

    col = lambda start, width: p_ref[:, start:start + width].astype(f32)
    o_gate, o_b, o_c, o_v = d_lru, 2 * d_lru, 2 * d_lru + d_sc, 2 * d_lru + 2 * d_sc
    o_gl = 2 * d_lru + 3 * d_sc
    o_gs = o_gl + d_model

    cv = col(o_c, d_sc) * col(o_v, d_sc)
    cvbuf[H:H + tm, :] = cv
    scw = scw_ref[...]
    conv = (scw[2:3, :] * cv + scw[1:2, :] * cvbuf[H - 1:H - 1 + tm, :]
            + scw[0:1, :] * cvbuf[H - 2:H - 2 + tm, :])
    s_in = (col(o_b, d_sc) * conv).astype(bf16)
    acc_s[...] = jnp.dot(s_in, wso_ref[...], preferred_element_type=f32)

    def lru_group(k):
        sl = slice(k * grp, (k + 1) * grp)
        lru_x = col(k * grp, grp)
        xbuf[H:H + tm, sl] = lru_x
        xc = (cw_ref[3:4, sl] * lru_x + cw_ref[2:3, sl] * xbuf[H - 1:H - 1 + tm, sl]
              + cw_ref[1:2, sl] * xbuf[H - 2:H - 2 + tm, sl]
              + cw_ref[0:1, sl] * xbuf[H - 3:H - 3 + tm, sl] + cb_ref[:, sl])
        z = jnp.dot(xc.astype(bf16), wg_ref[k], preferred_element_type=f32)
        r = _sigmoid(z[:, :grp] + ba_ref[:, sl])
        ig = _sigmoid(z[:, grp:] + bx_ref[:, sl])
        log_a = (-LRU_C * jax.nn.softplus(-lam_ref[:, sl])) * r
        a = jnp.exp(log_a)
        mult = jnp.sqrt(-jnp.tanh(log_a) * (a * a + 1.0))
        sa[pad:pad + tm, sl] = a
        sb[pad:pad + tm, sl] = mult * (ig * xc)

        s = 1
        while s < tm:
            for lo in range(tm - chunk, -1, -chunk):
                hi = pl.ds(pad + lo, chunk)
                sh = pl.ds(pad + lo - s, chunk)
                a_hi = sa[hi, sl]
                new_a = a_hi * sa[sh, sl]
                new_b = a_hi * sb[sh, sl] + sb[hi, sl]
                sa[hi, sl] = new_a
                sb[hi, sl] = new_b
            s *= 2
        hseq = sa[pad:pad + tm, sl] * hstate[0:1, sl] + sb[pad:pad + tm, sl]
        hstate[:, sl] = jnp.broadcast_to(hseq[tm - 1:tm, :], (H, grp))
        return (_gelu_tanh(col(o_gate + k * grp, grp)) * hseq).astype(bf16)

    n_parts = 2
    per_part = n_groups // n_parts
    for part in range(n_parts):
        y = jnp.concatenate([lru_group(k) for k in range(part * per_part, (part + 1) * per_part)],
                            axis=1)
        rows = slice(part * per_part * grp, (part + 1) * per_part * grp)
        d = jnp.dot(y, wlo_ref[rows, :], preferred_element_type=f32)
        if part == 0:
            acc_l[...] = d
        else:
            acc_l[...] += d

    n_blk = 4
    blk = d_model // n_blk
    for nb in range(n_blk):
        cs = slice(nb * blk, (nb + 1) * blk)
        merged = (_sigmoid(col(o_gl + nb * blk, blk)) * acc_l[:, cs]
                  + _sigmoid(col(o_gs + nb * blk, blk)) * acc_s[:, cs]).astype(bf16)
        d = jnp.dot(merged, wo_ref[cs, :], preferred_element_type=f32)
        if nb == 0:
            o_ref[...] = x_ref[...] + d
        else:
            o_ref[...] += d


def _mixer(p, x2, cw, cb, wg, ba, bx, lam, wlo, scw, wso, wo, *, tm, seq):
    n, d_model = x2.shape
    cols = p.shape[1]
    d_lru = cw.shape[1]
    d_sc = scw.shape[1]
    n_groups, grp, _ = wg.shape
    pad = tm // 2
    chunk = min(tm, 64)
    const = lambda shape: pl.BlockSpec(shape, lambda i: (0,) * len(shape),
                                       pipeline_mode=pl.Buffered(1))
    kern = functools.partial(
        _mixer_kernel, tm=tm, pad=pad, chunk=chunk, d_lru=d_lru, d_sc=d_sc, d_model=d_model,
        tiles_per_seq=seq // tm, n_groups=n_groups, grp=grp)
    return pl.pallas_call(
        kern,
        grid=(n // tm,),
        in_specs=[
            pl.BlockSpec((tm, cols), lambda i: (i, 0)),
            pl.BlockSpec((tm, d_model), lambda i: (i, 0)),
            const(cw.shape), const(cb.shape), const(wg.shape), const(ba.shape),
            const(bx.shape), const(lam.shape), const(wlo.shape), const(scw.shape),
            const(wso.shape), const(wo.shape),
        ],
        out_specs=pl.BlockSpec((tm, d_model), lambda i: (i, 0)),
        out_shape=jax.ShapeDtypeStruct((n, d_model), jnp.float32),
        scratch_shapes=[
            pltpu.VMEM((tm + SUBLANES, d_lru), jnp.float32),
            pltpu.VMEM((tm + SUBLANES, d_sc), jnp.float32),
            pltpu.VMEM((pad + tm, d_lru), jnp.float32),
            pltpu.VMEM((pad + tm, d_lru), jnp.float32),
            pltpu.VMEM((SUBLANES, d_lru), jnp.float32),
            pltpu.VMEM((tm, d_model), jnp.float32),
            pltpu.VMEM((tm, d_model), jnp.float32),
        ],
        compiler_params=pltpu.CompilerParams(
            dimension_semantics=("arbitrary",),
            vmem_limit_bytes=VMEM_LIMIT_BYTES),
        name="mixer",
    )(p, x2, cw, cb, wg, ba, bx, lam, wlo, scw, wso, wo)


def _ffn_kernel(x_ref, g_ref, wg_ref, wv_ref, cwg_ref, cwv_ref, wd_ref, gf_ref, o_ref,
                h_ref, ubuf_g, ubuf_v, halo_g, halo_v, act_even, act_odd,
                *, tm, tiles_per_seq, nj, n_tiles):
    t = pl.program_id(0)
    H = SUBLANES
    t_up = jnp.minimum(t, n_tiles - 1)
    j_up = t_up % nj
    seq_start = ((t_up // nj) % tiles_per_seq) == 0
    j_down = jnp.maximum(t - 1, 0) % nj

    @pl.when(t == 0)
    def _():
        act_odd[...] = jnp.zeros_like(act_odd)

    @pl.when(jnp.logical_and(j_up == 0, t < n_tiles))
    def _():
        h_ref[...] = _rmsnorm(x_ref[...], g_ref[...]).astype(jnp.bfloat16)

    @pl.when(j_down == 0)
    def _():
        o_ref[...] = x_ref[...]

    @pl.when(seq_start)
    def _():
        ubuf_g[0:H, :] = jnp.zeros((H, ubuf_g.shape[1]), jnp.float32)
        ubuf_v[0:H, :] = jnp.zeros((H, ubuf_v.shape[1]), jnp.float32)

    @pl.when(jnp.logical_not(seq_start))
    def _():
        ubuf_g[0:H, :] = halo_g[j_up]
        ubuf_v[0:H, :] = halo_v[j_up]

    def conv_branch(u, cw_ref, ubuf, halo):
        ubuf[H:H + tm, :] = u
        halo[j_up] = u[tm - H:tm, :]
        cw = cw_ref[...]
        return (cw[2:3, :] * u + cw[1:2, :] * ubuf[H - 1:H - 1 + tm, :]
                + cw[0:1, :] * ubuf[H - 2:H - 2 + tm, :])

    def stages(act_prev, act_cur):
        hb = h_ref[...]
        ug = jnp.dot(hb, wg_ref[...], preferred_element_type=jnp.float32)
        uv = jnp.dot(hb, wv_ref[...], preferred_element_type=jnp.float32)
        o_ref[...] += jnp.dot(act_prev[...], wd_ref[...], preferred_element_type=jnp.float32)
        cg = conv_branch(ug, cwg_ref, ubuf_g, halo_g)
        cv = conv_branch(uv, cwv_ref, ubuf_v, halo_v)
        act_cur[...] = (cg * _sigmoid(cg) * cv).astype(jnp.bfloat16)

    @pl.when(t % 2 == 0)
    def _():
        stages(act_odd, act_even)

    @pl.when(t % 2 == 1)
    def _():
        stages(act_even, act_odd)

    @pl.when(jnp.logical_and(j_down == nj - 1, t > 0))
    def _():
        o_ref[...] = _rmsnorm(o_ref[...], gf_ref[...])


def _conv_ffn(x1, g, w_up, conv_w, w_down, g_final, *, tm, tn, seq):
    n, d = x1.shape
    d_ff = w_down.shape[0]
    nj = d_ff // tn
    n_tiles = (n // tm) * nj
    kern = functools.partial(_ffn_kernel, tm=tm, tiles_per_seq=seq // tm, nj=nj,
                             n_tiles=n_tiles)
    up = lambda t: jnp.minimum(t, n_tiles - 1)
    down = lambda t: jnp.maximum(t - 1, 0)
    return pl.pallas_call(
        kern,
        grid=(n_tiles + 1,),
        in_specs=[
            pl.BlockSpec((tm, d), lambda t: (up(t) // nj, 0)),
            pl.BlockSpec((1, d), lambda t: (0, 0)),
            pl.BlockSpec((d, tn), lambda t: (0, up(t) % nj)),
            pl.BlockSpec((d, tn), lambda t: (0, up(t) % nj + nj)),
            pl.BlockSpec((3, tn), lambda t: (0, up(t) % nj)),
            pl.BlockSpec((3, tn), lambda t: (0, up(t) % nj + nj)),
            pl.BlockSpec((tn, d), lambda t: (down(t) % nj, 0)),
            pl.BlockSpec((1, d), lambda t: (0, 0)),
        ],
        out_specs=pl.BlockSpec((tm, d), lambda t: (down(t) // nj, 0)),
        out_shape=jax.ShapeDtypeStruct((n, d), jnp.float32),
        scratch_shapes=[
            pltpu.VMEM((tm, d), jnp.bfloat16),
            pltpu.VMEM((tm + SUBLANES, tn), jnp.float32),
            pltpu.VMEM((tm + SUBLANES, tn), jnp.float32),
            pltpu.VMEM((nj, SUBLANES, tn), jnp.float32),
            pltpu.VMEM((nj, SUBLANES, tn), jnp.float32),
            pltpu.VMEM((tm, tn), jnp.bfloat16),
            pltpu.VMEM((tm, tn), jnp.bfloat16),
        ],
        compiler_params=pltpu.CompilerParams(
            dimension_semantics=("arbitrary",),
            vmem_limit_bytes=VMEM_LIMIT_BYTES),
        name="conv_ffn",
    )(x1, g, w_up, w_up, conv_w, conv_w, w_down, g_final)


def _block_diag_gates(wa, wx, heads_per_group):
    n_heads, hd, _ = wa.shape
    n_groups = n_heads // heads_per_group
    eye = jnp.eye(heads_per_group, dtype=wa.dtype)

    def bd(w):
        w = w.reshape(n_groups, heads_per_group, hd, hd)
        full = jnp.einsum('ghij,hk->ghikj', w, eye)
        return full.reshape(n_groups, heads_per_group * hd, heads_per_group * hd)

    return jnp.concatenate([bd(wa), bd(wx)], axis=-1)


def kernel(x, g_mix, w_in, lru_conv_w, lru_conv_b, lru_wa, lru_ba, lru_wx, lru_bx, lru_lambda, lru_w_out, sc_conv_w, sc_w_out, w_o, g_ffn, ffn_w_up, ffn_conv_w, ffn_w_down, g_final):
    bsz, seq, d = x.shape
    n = bsz * seq
    bf = jnp.bfloat16
    row = lambda v: v.reshape(1, -1)
    x2 = x.reshape(n, d)

    p, (wlo, wso, wo, w_up, w_down) = _in_proj(
        x2, row(g_mix), w_in.astype(bf), [lru_w_out, sc_w_out, w_o, ffn_w_up, ffn_w_down],
        tm=1024, tn=1024)

    mxu_width = 256
    heads_per_group = max(1, mxu_width // lru_wa.shape[1])
    wg = _block_diag_gates(lru_wa, lru_wx, heads_per_group).astype(bf)
    x1 = _mixer(p, x2, lru_conv_w, row(lru_conv_b), wg, row(lru_ba), row(lru_bx),
                row(lru_lambda), wlo, sc_conv_w, wso, wo, tm=256, seq=seq)

    out = _conv_ffn(x1, row(g_ffn), w_up, ffn_conv_w, w_down, row(g_final),
                    tm=1024, tn=512, seq=seq)
    return out.reshape(bsz, seq, d)
```

```python
import functools

import jax
import jax.numpy as jnp
from jax import lax
from jax.experimental import pallas as pl
from jax.experimental.pallas import tpu as pltpu

EPS = 1e-6
LRU_C = 8.0
SUBLANES = 8
BF16_SUBLANES = 16
VMEM_LIMIT_BYTES = 60 * 1024 * 1024


def _rmsnorm(x, g):
    ms = jnp.mean(x * x, axis=-1, keepdims=True)
    return x * lax.rsqrt(ms + EPS) * g


def _sigmoid(x):
    return 0.5 * jnp.tanh(0.5 * x) + 0.5


def _gelu_tanh(x):
    c = 0.7978845608028654
    return 0.5 * x * (1.0 + jnp.tanh(c * (x + 0.044715 * (x * x * x))))


def _in_proj_kernel(x_ref, g_ref, w_ref, *refs, n_side):
    side_in = refs[:n_side]
    o_ref = refs[n_side]
    side_out = refs[n_side + 1:2 * n_side + 1]
    h_ref = refs[2 * n_side + 1]

    @pl.when(pl.program_id(1) == 0)
    def _():
        h_ref[...] = _rmsnorm(x_ref[...], g_ref[...]).astype(jnp.bfloat16)

    o_ref[...] = jnp.dot(h_ref[...], w_ref[...],
                         preferred_element_type=jnp.float32).astype(o_ref.dtype)
    for src, dst in zip(side_in, side_out):
        dst[...] = src[...].astype(jnp.bfloat16)


def _in_proj(x2, g, w_bf16, side_weights, *, tm, tn):
    n, d = x2.shape
    cols = w_bf16.shape[1]
    ni, nj = n // tm, cols // tn
    side_specs, side_shapes = [], []
    for w in side_weights:
        rows = w.shape[0]
        per_row_tile = max(c for c in range(1, nj + 1)
                           if rows % (ni * c) == 0 and (rows // (ni * c)) % BF16_SUBLANES == 0)
        blk = rows // (ni * per_row_tile)
        stride = nj // per_row_tile
        imap = functools.partial(
            lambda i, j, c, s: (i * c + jnp.minimum(j // s, c - 1), 0), c=per_row_tile, s=stride)
        side_specs.append(pl.BlockSpec((blk, w.shape[1]), imap))
        side_shapes.append(jax.ShapeDtypeStruct(w.shape, jnp.bfloat16))
    outs = pl.pallas_call(
        functools.partial(_in_proj_kernel, n_side=len(side_weights)),
        grid=(ni, nj),
        in_specs=[
            pl.BlockSpec((tm, d), lambda i, j: (i, 0)),
            pl.BlockSpec((1, d), lambda i, j: (0, 0)),
            pl.BlockSpec((d, tn), lambda i, j: (0, j)),
        ] + side_specs,
        out_specs=[pl.BlockSpec((tm, tn), lambda i, j: (i, j))] + side_specs,
        out_shape=[jax.ShapeDtypeStruct((n, cols), jnp.bfloat16)] + side_shapes,
        scratch_shapes=[pltpu.VMEM((tm, d), jnp.bfloat16)],
        compiler_params=pltpu.CompilerParams(
            dimension_semantics=("arbitrary", "arbitrary"),
            vmem_limit_bytes=VMEM_LIMIT_BYTES),
        name="in_proj",
    )(x2, g, w_bf16, *side_weights)
    return outs[0], outs[1:]


def _mixer_kernel(p_ref, x_ref, cw_ref, cb_ref, wg_ref, ba_ref, bx_ref, lam_ref,
                  wlo_ref, scw_ref, wso_ref, wo_ref, o_ref,
                  xbuf, cvbuf, sa, sb, hstate,
                  *, tm, pad, chunk, d_lru, d_sc, d_model, tiles_per_seq, n_groups, grp):
    i = pl.program_id(0)
    seq_start = (i % tiles_per_seq) == 0
    H = SUBLANES
    f32 = jnp.float32
    bf16 = jnp.bfloat16

    @pl.when(i == 0)
    def _():
        sa[0:pad, :] = jnp.ones((pad, d_lru), f32)
        sb[0:pad, :] = jnp.zeros((pad, d_lru), f32)

    @pl.when(jnp.logical_not(seq_start))
    def _():
        xbuf[0:H, :] = xbuf[tm:tm + H, :]
        cvbuf[0:H, :] = cvbuf[tm:tm + H, :]

    @pl.when(seq_start)
    def _():
        xbuf[0:H, :] = jnp.zeros((H, d_lru), f32)
        cvbuf[0:H, :] = jnp.zeros((H, d_sc), f32)
        hstate[...] = jnp.zeros_like(hstate)

    col = lambda start, width: p_ref[:, start:start + width].astype(f32)
    o_gate, o_b, o_c, o_v = d_lru, 2 * d_lru, 2 * d_lru + d_sc, 2 * d_lru + 2 * d_sc
    o_gl = 2 * d_lru + 3 * d_sc
    o_gs = o_gl + d_model

    cv = col(o_c, d_sc) * col(o_v, d_sc)
    cvbuf[H:H + tm, :] = cv
    scw = scw_ref[...]
    conv = (scw[2:3, :] * cv + scw[1:2, :] * cvbuf[H - 1:H - 1 + tm, :]
            + scw[0:1, :] * cvbuf[H - 2:H - 2 + tm, :])
    s_in = (col(o_b, d_sc) * conv).astype(bf16)
    y_sc = jnp.dot(s_in, wso_ref[...], preferred_element_type=f32)

    lru_x = col(0, d_lru)
    xbuf[H:H + tm, :] = lru_x
    cw = cw_ref[...]
    xc = (cw[3:4, :] * lru_x + cw[2:3, :] * xbuf[H - 1:H - 1 + tm, :]
          + cw[1:2, :] * xbuf[H - 2:H - 2 + tm, :] + cw[0:1, :] * xbuf[H - 3:H - 3 + tm, :]
          + cb_ref[...])
    xcb = xc.astype(bf16)
    lam_scale = -LRU_C * jax.nn.softplus(-lam_ref[...])
    for k in range(n_groups):
        sl = slice(k * grp, (k + 1) * grp)
        z = jnp.dot(xcb[:, sl], wg_ref[k], preferred_element_type=f32)
        r = _sigmoid(z[:, :grp] + ba_ref[:, sl])
        ig = _sigmoid(z[:, grp:] + bx_ref[:, sl])
        log_a = lam_scale[:, sl] * r
        a = jnp.exp(log_a)
        mult = jnp.sqrt(-jnp.tanh(log_a) * (a * a + 1.0))
        sa[pad:pad + tm, sl] = a
        sb[pad:pad + tm, sl] = mult * (ig * xc[:, sl])

    s = 1
    while s < H:
        for lo in range(tm - chunk, -1, -chunk):
            hi = pl.ds(pad + lo, chunk)
            sh = pl.ds(pad + lo - s, chunk)
            a_hi = sa[hi, :]
            new_a = a_hi * sa[sh, :]
            new_b = a_hi * sb[sh, :] + sb[hi, :]
            sa[hi, :] = new_a
            sb[hi, :] = new_b
        s *= 2
    h = jnp.broadcast_to(hstate[0:1, :], (H, d_lru))
    for g in range(tm // H):
        rows = pl.ds(pad + g * H, H)
        h = sa[rows, :] * h + sb[rows, :]
        sb[rows, :] = h
    hstate[...] = jnp.broadcast_to(h[H - 1:H, :], (H, d_lru))

    y_in = (_gelu_tanh(col(o_gate, d_lru)) * sb[pad:pad + tm, :]).astype(bf16)
    y_lru = jnp.dot(y_in, wlo_ref[...], preferred_element_type=f32)

    merged = _sigmoid(col(o_gl, d_model)) * y_lru + _sigmoid(col(o_gs, d_model)) * y_sc
    o_ref[...] = x_ref[...] + jnp.dot(merged.astype(bf16), wo_ref[...],
                                      preferred_element_type=f32)


def _mixer(p, x2, cw, cb, wg, ba, bx, lam, wlo, scw, wso, wo, *, tm, seq):
    n, d_model = x2.shape
    cols = p.shape[1]
    d_lru = cw.shape[1]
    d_sc = scw.shape[1]
    n_groups, grp, _ = wg.shape
    pad = SUBLANES
    chunk = min(tm, 64)
    const = lambda shape: pl.BlockSpec(shape, lambda i: (0,) * len(shape),
                                       pipeline_mode=pl.Buffered(1))
    kern = functools.partial(
        _mixer_kernel, tm=tm, pad=pad, chunk=chunk, d_lru=d_lru, d_sc=d_sc, d_model=d_model,
        tiles_per_seq=seq // tm, n_groups=n_groups, grp=grp)
    return pl.pallas_call(
        kern,
        grid=(n // tm,),
        in_specs=[
            pl.BlockSpec((tm, cols), lambda i: (i, 0)),
            pl.BlockSpec((tm, d_model), lambda i: (i, 0)),
            const(cw.shape), const(cb.shape), const(wg.shape), const(ba.shape),
            const(bx.shape), const(lam.shape), const(wlo.shape), const(scw.shape),
            const(wso.shape), const(wo.shape),
        ],
        out_specs=pl.BlockSpec((tm, d_model), lambda i: (i, 0)),
        out_shape=jax.ShapeDtypeStruct((n, d_model), jnp.float32),
        scratch_shapes=[
            pltpu.VMEM((tm + SUBLANES, d_lru), jnp.float32),
            pltpu.VMEM((tm + SUBLANES, d_sc), jnp.float32),
            pltpu.VMEM((pad + tm, d_lru), jnp.float32),
            pltpu.VMEM((pad + tm, d_lru), jnp.float32),
            pltpu.VMEM((SUBLANES, d_lru), jnp.float32),
        ],
        compiler_params=pltpu.CompilerParams(
            dimension_semantics=("arbitrary",),
            vmem_limit_bytes=VMEM_LIMIT_BYTES),
        name="mixer",
    )(p, x2, cw, cb, wg, ba, bx, lam, wlo, scw, wso, wo)


def _ffn_kernel(x_ref, g_ref, wg_ref, wv_ref, cwg_ref, cwv_ref, wd_ref, gf_ref, o_ref,
                h_ref, ubuf_g, ubuf_v, halo_g, halo_v, act_even, act_odd,
                *, tm, tiles_per_seq, nj, n_tiles):
    t = pl.program_id(0)
    H = SUBLANES
    t_up = jnp.minimum(t, n_tiles - 1)
    j_up = t_up % nj
    seq_start = ((t_up // nj) % tiles_per_seq) == 0
    j_down = jnp.maximum(t - 1, 0) % nj

    @pl.when(t == 0)
    def _():
        act_odd[...] = jnp.zeros_like(act_odd)

    @pl.when(jnp.logical_and(j_up == 0, t < n_tiles))
    def _():
        h_ref[...] = _rmsnorm(x_ref[...], g_ref[...]).astype(jnp.bfloat16)

    @pl.when(j_down == 0)
    def _():
        o_ref[...] = x_ref[...]

    @pl.when(seq_start)
    def _():
        ubuf_g[0:H, :] = jnp.zeros((H, ubuf_g.shape[1]), jnp.float32)
        ubuf_v[0:H, :] = jnp.zeros((H, ubuf_v.shape[1]), jnp.float32)

    @pl.when(jnp.logical_not(seq_start))
    def _():
        ubuf_g[0:H, :] = halo_g[j_up]
        ubuf_v[0:H, :] = halo_v[j_up]

    def conv_branch(u, cw_ref, ubuf, halo):
        ubuf[H:H + tm, :] = u
        halo[j_up] = u[tm - H:tm, :]
        cw = cw_ref[...]
        return (cw[2:3, :] * u + cw[1:2, :] * ubuf[H - 1:H - 1 + tm, :]
                + cw[0:1, :] * ubuf[H - 2:H - 2 + tm, :])

    def stages(act_prev, act_cur):
        hb = h_ref[...]
        ug = jnp.dot(hb, wg_ref[...], preferred_element_type=jnp.float32)
        uv = jnp.dot(hb, wv_ref[...], preferred_element_type=jnp.float32)
        o_ref[...] += jnp.dot(act_prev[...], wd_ref[...], preferred_element_type=jnp.float32)
        cg = conv_branch(ug, cwg_ref, ubuf_g, halo_g)
        cv = conv_branch(uv, cwv_ref, ubuf_v, halo_v)
        act_cur[...] = (cg * _sigmoid(cg) * cv).astype(jnp.bfloat16)

    @pl.when(t % 2 == 0)
    def _():
        stages(act_odd, act_even)

    @pl.when(t % 2 == 1)
    def _():
        stages(act_even, act_odd)

    @pl.when(jnp.logical_and(j_down == nj - 1, t > 0))
    def _():
        o_ref[...] = _rmsnorm(o_ref[...], gf_ref[...])


def _conv_ffn(x1, g, w_up, conv_w, w_down, g_final, *, tm, tn, seq):
    n, d = x1.shape
    d_ff = w_down.shape[0]
    nj = d_ff // tn
    n_tiles = (n // tm) * nj
    kern = functools.partial(_ffn_kernel, tm=tm, tiles_per_seq=seq // tm, nj=nj,
                             n_tiles=n_tiles)
    up = lambda t: jnp.minimum(t, n_tiles - 1)
    down = lambda t: jnp.maximum(t - 1, 0)
    return pl.pallas_call(
        kern,
        grid=(n_tiles + 1,),
        in_specs=[
            pl.BlockSpec((tm, d), lambda t: (up(t) // nj, 0)),
            pl.BlockSpec((1, d), lambda t: (0, 0)),
            pl.BlockSpec((d, tn), lambda t: (0, up(t) % nj)),
            pl.BlockSpec((d, tn), lambda t: (0, up(t) % nj + nj)),
            pl.BlockSpec((3, tn), lambda t: (0, up(t) % nj)),
            pl.BlockSpec((3, tn), lambda t: (0, up(t) % nj + nj)),
            pl.BlockSpec((tn, d), lambda t: (down(t) % nj, 0)),
            pl.BlockSpec((1, d), lambda t: (0, 0)),
        ],
        out_specs=pl.BlockSpec((tm, d), lambda t: (down(t) // nj, 0)),
        out_shape=jax.ShapeDtypeStruct((n, d), jnp.float32),
        scratch_shapes=[
            pltpu.VMEM((tm, d), jnp.bfloat16),
            pltpu.VMEM((tm + SUBLANES, tn), jnp.float32),
            pltpu.VMEM((tm + SUBLANES, tn), jnp.float32),
            pltpu.VMEM((nj, SUBLANES, tn), jnp.float32),
            pltpu.VMEM((nj, SUBLANES, tn), jnp.float32),
            pltpu.VMEM((tm, tn), jnp.bfloat16),
            pltpu.VMEM((tm, tn), jnp.bfloat16),
        ],
        compiler_params=pltpu.CompilerParams(
            dimension_semantics=("arbitrary",),
            vmem_limit_bytes=VMEM_LIMIT_BYTES),
        name="conv_ffn",
    )(x1, g, w_up, w_up, conv_w, conv_w, w_down, g_final)


def _block_diag_gates(wa, wx, heads_per_group):
    n_heads, hd, _ = wa.shape
    n_groups = n_heads // heads_per_group
    eye = jnp.eye(heads_per_group, dtype=wa.dtype)

    def bd(w):
        w = w.reshape(n_groups, heads_per_group, hd, hd)
        full = jnp.einsum('ghij,hk->ghikj', w, eye)
        return full.reshape(n_groups, heads_per_group * hd, heads_per_group * hd)

    return jnp.concatenate([bd(wa), bd(wx)], axis=-1)


def kernel(x, g_mix, w_in, lru_conv_w, lru_conv_b, lru_wa, lru_ba, lru_wx, lru_bx, lru_lambda, lru_w_out, sc_conv_w, sc_w_out, w_o, g_ffn, ffn_w_up, ffn_conv_w, ffn_w_down, g_final):
    bsz, seq, d = x.shape
    n = bsz * seq
    bf = jnp.bfloat16
    row = lambda v: v.reshape(1, -1)
    x2 = x.reshape(n, d)

    p, (wlo, wso, wo, w_up, w_down) = _in_proj(
        x2, row(g_mix), w_in.astype(bf), [lru_w_out, sc_w_out, w_o, ffn_w_up, ffn_w_down],
        tm=1024, tn=1536)

    mxu_width = 256
    heads_per_group = max(1, mxu_width // lru_wa.shape[1])
    wg = _block_diag_gates(lru_wa, lru_wx, heads_per_group).astype(bf)
    x1 = _mixer(p, x2, lru_conv_w, row(lru_conv_b), wg, row(lru_ba), row(lru_bx),
                row(lru_lambda), wlo, sc_conv_w, wso, wo, tm=256, seq=seq)

    out = _conv_ffn(x1, row(g_ffn), w_up, ffn_conv_w, w_down, row(g_final),
                    tm=1024, tn=512, seq=seq)
    return out.reshape(bsz, seq, d)
```

```python
import functools

import jax
import jax.numpy as jnp
from jax import lax
from jax.experimental import pallas as pl
from jax.experimental.pallas import tpu as pltpu

EPS = 1e-6
LRU_C = 8.0
SUBLANES = 8
BF16_SUBLANES = 16
VMEM_LIMIT_BYTES = 60 * 1024 * 1024


def _rmsnorm(x, g):
    ms = jnp.mean(x * x, axis=-1, keepdims=True)
    return x * lax.rsqrt(ms + EPS) * g


def _sigmoid(x):
    return 0.5 * jnp.tanh(0.5 * x) + 0.5


def _gelu_tanh(x):
    c = 0.7978845608028654
    return 0.5 * x * (1.0 + jnp.tanh(c * (x + 0.044715 * (x * x * x))))


def _in_proj_first_kernel(x_ref, g_ref, w_ref, o_ref, wb_ref, h_ref):
    @pl.when(pl.program_id(0) == 0)
    def _():
        h_ref[...] = _rmsnorm(x_ref[...], g_ref[...]).astype(jnp.bfloat16)

    wb = w_ref[...].astype(jnp.bfloat16)
    wb_ref[...] = wb
    o_ref[...] = jnp.dot(h_ref[...], wb, preferred_element_type=jnp.float32).astype(o_ref.dtype)


def _in_proj_first(x2, g, w, *, tm, tn):
    n, d = x2.shape
    cols = w.shape[1]
    return pl.pallas_call(
        _in_proj_first_kernel,
        grid=(cols // tn,),
        in_specs=[
            pl.BlockSpec((tm, d), lambda j: (0, 0)),
            pl.BlockSpec((1, d), lambda j: (0, 0)),
            pl.BlockSpec((d, tn), lambda j: (0, j)),
        ],
        out_specs=[pl.BlockSpec((tm, tn), lambda j: (0, j)),
                   pl.BlockSpec((d, tn), lambda j: (0, j))],
        out_shape=[jax.ShapeDtypeStruct((tm, cols), jnp.bfloat16),
                   jax.ShapeDtypeStruct((d, cols), jnp.bfloat16)],
        scratch_shapes=[pltpu.VMEM((tm, d), jnp.bfloat16)],
        compiler_params=pltpu.CompilerParams(
            dimension_semantics=("arbitrary",),
            vmem_limit_bytes=VMEM_LIMIT_BYTES),
        name="in_proj_first",
    )(x2, g, w)


def _in_proj_kernel(x_ref, g_ref, w_ref, p0_ref, *refs, n_side):
    side_in = refs[:n_side]
    o_ref = refs[n_side]
    side_out = refs[n_side + 1:2 * n_side + 1]
    h_ref = refs[2 * n_side + 1]
    i = pl.program_id(0)

    @pl.when(i == 0)
    def _():
        o_ref[...] = p0_ref[...]

    @pl.when(jnp.logical_and(i > 0, pl.program_id(1) == 0))
    def _():
        h_ref[...] = _rmsnorm(x_ref[...], g_ref[...]).astype(jnp.bfloat16)

    @pl.when(i > 0)
    def _():
        o_ref[...] = jnp.dot(h_ref[...], w_ref[...],
                             preferred_element_type=jnp.float32).astype(o_ref.dtype)

    for src, dst in zip(side_in, side_out):
        dst[...] = src[...].astype(jnp.bfloat16)


def _in_proj(x2, g, w_bf16, p0, side_weights, *, tm, tn):
    n, d = x2.shape
    cols = w_bf16.shape[1]
    ni, nj = n // tm, cols // tn
    side_specs, side_shapes = [], []
    for w in side_weights:
        rows = w.shape[0]
        per_row_tile = max(c for c in range(1, nj + 1)
                           if rows % (ni * c) == 0 and (rows // (ni * c)) % BF16_SUBLANES == 0)
        blk = rows // (ni * per_row_tile)
        stride = nj // per_row_tile
        imap = functools.partial(
            lambda i, j, c, s: (i * c + jnp.minimum(j // s, c - 1), 0), c=per_row_tile, s=stride)
        side_specs.append(pl.BlockSpec((blk, w.shape[1]), imap))
        side_shapes.append(jax.ShapeDtypeStruct(w.shape, jnp.bfloat16))
    outs = pl.pallas_call(
        functools.partial(_in_proj_kernel, n_side=len(side_weights)),
        grid=(ni, nj),
        in_specs=[
            pl.BlockSpec((tm, d), lambda i, j: (jnp.maximum(i, 1), 0)),
            pl.BlockSpec((1, d), lambda i, j: (0, 0)),
            pl.BlockSpec((d, tn), lambda i, j: (0, j)),
            pl.BlockSpec((tm, tn), lambda i, j: (0, jnp.where(i == 0, j, nj - 1))),
        ] + side_specs,
        out_specs=[pl.BlockSpec((tm, tn), lambda i, j: (i, j))] + side_specs,
        out_shape=[jax.ShapeDtypeStruct((n, cols), jnp.bfloat16)] + side_shapes,
        scratch_shapes=[pltpu.VMEM((tm, d), jnp.bfloat16)],
        compiler_params=pltpu.CompilerParams(
            dimension_semantics=("arbitrary", "arbitrary"),
            vmem_limit_bytes=VMEM_LIMIT_BYTES),
        name="in_proj",
    )(x2, g, w_bf16, p0, *side_weights)
    return outs[0], outs[1:]


def _mixer_kernel(p_ref, x_ref, cw_ref, cb_ref, wg_ref, ba_ref, bx_ref, lam_ref,
                  wlo_ref, scw_ref, wso_ref, wo_ref, side_a_ref, side_b_ref,
                  o_ref, side_a_out, side_b_out,
                  xbuf, cvbuf, sa, sb, hstate,
                  *, tm, pad, chunk, d_lru, d_sc, d_model, tiles_per_seq, n_groups, grp):
    i = pl.program_id(0)
    side_a_out[...] = side_a_ref[...].astype(jnp.bfloat16)
    side_b_out[...] = side_b_ref[...].astype(jnp.bfloat16)
    seq_start = (i % tiles_per_seq) == 0
    H = SUBLANES
    f32 = jnp.float32
    bf16 = jnp.bfloat16

    @pl.when(i == 0)
    def _():
        sa[0:pad, :] = jnp.ones((pad, d_lru), f32)
        sb[0:pad, :] = jnp.zeros((pad, d_lru), f32)

    @pl.when(jnp.logical_not(seq_start))
    def _():
        xbuf[0:H, :] = xbuf[tm:tm + H, :]
        cvbuf[0:H, :] = cvbuf[tm:tm + H, :]

    @pl.when(seq_start)
    def _():
        xbuf[0:H, :] = jnp.zeros((H, d_lru), f32)
        cvbuf[0:H, :] = jnp.zeros((H, d_sc), f32)
        hstate[...] = jnp.zeros_like(hstate)

    col = lambda start, width: p_ref[:, start:start + width].astype(f32)
    o_gate, o_b, o_c, o_v = d_lru, 2 * d_lru, 2 * d_lru + d_sc, 2 * d_lru + 2 * d_sc
    o_gl = 2 * d_lru + 3 * d_sc
    o_gs = o_gl + d_model

    cv = col(o_c, d_sc) * col(o_v, d_sc)
    cvbuf[H:H + tm, :] = cv
    scw = scw_ref[...]
    conv = (scw[2:3, :] * cv + scw[1:2, :] * cvbuf[H - 1:H - 1 + tm, :]
            + scw[0:1, :] * cvbuf[H - 2:H - 2 + tm, :])
    s_in = (col(o_b, d_sc) * conv).astype(bf16)
    y_sc = jnp.dot(s_in, wso_ref[...], preferred_element_type=f32)

    lru_x = col(0, d_lru)
    xbuf[H:H + tm, :] = lru_x
    cw = cw_ref[...]
    xc = (cw[3:4, :] * lru_x + cw[2:3, :] * xbuf[H - 1:H - 1 + tm, :]
          + cw[1:2, :] * xbuf[H - 2:H - 2 + tm, :] + cw[0:1, :] * xbuf[H - 3:H - 3 + tm, :]
          + cb_ref[...])
    xcb = xc.astype(bf16)
    lam_scale = -LRU_C * jax.nn.softplus(-lam_ref[...])
    for k in range(n_groups):
        sl = slice(k * grp, (k + 1) * grp)
        z = jnp.dot(xcb[:, sl], wg_ref[k], preferred_element_type=f32)
        r = _sigmoid(z[:, :grp] + ba_ref[:, sl])
        ig = _sigmoid(z[:, grp:] + bx_ref[:, sl])
        log_a = lam_scale[:, sl] * r
        a = jnp.exp(log_a)
        mult = jnp.sqrt(-jnp.tanh(log_a) * (a * a + 1.0))
        sa[pad:pad + tm, sl] = a
        sb[pad:pad + tm, sl] = mult * (ig * xc[:, sl])

    s = 1
    while s < H:
        for lo in range(tm - chunk, -1, -chunk):
            hi = pl.ds(pad + lo, chunk)
            sh = pl.ds(pad + lo - s, chunk)
            a_hi = sa[hi, :]
            new_a = a_hi * sa[sh, :]
            new_b = a_hi * sb[sh, :] + sb[hi, :]
            sa[hi, :] = new_a
            sb[hi, :] = new_b
        s *= 2
    h = jnp.broadcast_to(hstate[0:1, :], (H, d_lru))
    for g in range(tm // H):
        rows = pl.ds(pad + g * H, H)
        h = sa[rows, :] * h + sb[rows, :]
        sb[rows, :] = h
    hstate[...] = jnp.broadcast_to(h[H - 1:H, :], (H, d_lru))

    y_in = (_gelu_tanh(col(o_gate, d_lru)) * sb[pad:pad + tm, :]).astype(bf16)
    y_lru = jnp.dot(y_in, wlo_ref[...], preferred_element_type=f32)

    merged = _sigmoid(col(o_gl, d_model)) * y_lru + _sigmoid(col(o_gs, d_model)) * y_sc
    o_ref[...] = x_ref[...] + jnp.dot(merged.astype(bf16), wo_ref[...],
                                      preferred_element_type=f32)


def _mixer(p, x2, cw, cb, wg, ba, bx, lam, wlo, scw, wso, wo, side_weights, *, tm, seq):
    n, d_model = x2.shape
    n_tiles = n // tm
    side_specs = [pl.BlockSpec((w.shape[0] // n_tiles, w.shape[1]), lambda i: (i, 0))
                  for w in side_weights]
    assert all(w.shape[0] % (n_tiles * BF16_SUBLANES) == 0 for w in side_weights)
    cols = p.shape[1]
    d_lru = cw.shape[1]
    d_sc = scw.shape[1]
    n_groups, grp, _ = wg.shape
    pad = SUBLANES
    chunk = min(tm, 64)
    const = lambda shape: pl.BlockSpec(shape, lambda i: (0,) * len(shape),
                                       pipeline_mode=pl.Buffered(1))
    kern = functools.partial(
        _mixer_kernel, tm=tm, pad=pad, chunk=chunk, d_lru=d_lru, d_sc=d_sc, d_model=d_model,
        tiles_per_seq=seq // tm, n_groups=n_groups, grp=grp)
    return pl.pallas_call(
        kern,
        grid=(n // tm,),
        in_specs=[
            pl.BlockSpec((tm, cols), lambda i: (i, 0)),
            pl.BlockSpec((tm, d_model), lambda i: (i, 0)),
            const(cw.shape), const(cb.shape), const(wg.shape), const(ba.shape),
            const(bx.shape), const(lam.shape), const(wlo.shape), const(scw.shape),
            const(wso.shape), const(wo.shape),
        ] + side_specs,
        out_specs=[pl.BlockSpec((tm, d_model), lambda i: (i, 0))] + side_specs,
        out_shape=[jax.ShapeDtypeStruct((n, d_model), jnp.float32)]
        + [jax.ShapeDtypeStruct(w.shape, jnp.bfloat16) for w in side_weights],
        scratch_shapes=[
            pltpu.VMEM((tm + SUBLANES, d_lru), jnp.float32),
            pltpu.VMEM((tm + SUBLANES, d_sc), jnp.float32),
            pltpu.VMEM((pad + tm, d_lru), jnp.float32),
            pltpu.VMEM((pad + tm, d_lru), jnp.float32),
            pltpu.VMEM((SUBLANES, d_lru), jnp.float32),
        ],
        compiler_params=pltpu.CompilerParams(
            dimension_semantics=("arbitrary",),
            vmem_limit_bytes=VMEM_LIMIT_BYTES),
        name="mixer",
    )(p, x2, cw, cb, wg, ba, bx, lam, wlo, scw, wso, wo, *side_weights)


def _ffn_kernel(x_ref, g_ref, wg_ref, wv_ref, cwg_ref, cwv_ref, wd_ref, gf_ref, o_ref,
                h_ref, ubuf_g, ubuf_v, halo_g, halo_v, act_even, act_odd,
                *, tm, tiles_per_seq, nj, n_tiles):
    t = pl.program_id(0)
    H = SUBLANES
    t_up = jnp.minimum(t, n_tiles - 1)
    j_up = t_up % nj
    seq_start = ((t_up // nj) % tiles_per_seq) == 0
    j_down = jnp.maximum(t - 1, 0) % nj

    @pl.when(t == 0)
    def _():
        act_odd[...] = jnp.zeros_like(act_odd)

    @pl.when(jnp.logical_and(j_up == 0, t < n_tiles))
    def _():
        h_ref[...] = _rmsnorm(x_ref[...], g_ref[...]).astype(jnp.bfloat16)

    @pl.when(j_down == 0)
    def _():
        o_ref[...] = x_ref[...]

    @pl.when(seq_start)
    def _():
        ubuf_g[0:H, :] = jnp.zeros((H, ubuf_g.shape[1]), jnp.float32)
        ubuf_v[0:H, :] = jnp.zeros((H, ubuf_v.shape[1]), jnp.float32)

    @pl.when(jnp.logical_not(seq_start))
    def _():
        ubuf_g[0:H, :] = halo_g[j_up]
        ubuf_v[0:H, :] = halo_v[j_up]

    def conv_branch(u, cw_ref, ubuf, halo):
        ubuf[H:H + tm, :] = u
        halo[j_up] = u[tm - H:tm, :]
        cw = cw_ref[...]
        return (cw[2:3, :] * u + cw[1:2, :] * ubuf[H - 1:H - 1 + tm, :]
                + cw[0:1, :] * ubuf[H - 2:H - 2 + tm, :])

    def stages(act_prev, act_cur):
        hb = h_ref[...]
        ug = jnp.dot(hb, wg_ref[...], preferred_element_type=jnp.float32)
        uv = jnp.dot(hb, wv_ref[...], preferred_element_type=jnp.float32)
        o_ref[...] += jnp.dot(act_prev[...], wd_ref[...], preferred_element_type=jnp.float32)
        cg = conv_branch(ug, cwg_ref, ubuf_g, halo_g)
        cv = conv_branch(uv, cwv_ref, ubuf_v, halo_v)
        act_cur[...] = (cg * _sigmoid(cg) * cv).astype(jnp.bfloat16)

    @pl.when(t % 2 == 0)
    def _():
        stages(act_odd, act_even)

    @pl.when(t % 2 == 1)
    def _():
        stages(act_even, act_odd)

    @pl.when(jnp.logical_and(j_down == nj - 1, t > 0))
    def _():
        o_ref[...] = _rmsnorm(o_ref[...], gf_ref[...])


def _conv_ffn(x1, g, w_up, conv_w, w_down, g_final, *, tm, tn, seq):
    n, d = x1.shape
    d_ff = w_down.shape[0]
    nj = d_ff // tn
    n_tiles = (n // tm) * nj
    kern = functools.partial(_ffn_kernel, tm=tm, tiles_per_seq=seq // tm, nj=nj,
                             n_tiles=n_tiles)
    up = lambda t: jnp.minimum(t, n_tiles - 1)
    down = lambda t: jnp.maximum(t - 1, 0)
    return pl.pallas_call(
        kern,
        grid=(n_tiles + 1,),
        in_specs=[
            pl.BlockSpec((tm, d), lambda t: (up(t) // nj, 0)),
            pl.BlockSpec((1, d), lambda t: (0, 0)),
            pl.BlockSpec((d, tn), lambda t: (0, up(t) % nj)),
            pl.BlockSpec((d, tn), lambda t: (0, up(t) % nj + nj)),
            pl.BlockSpec((3, tn), lambda t: (0, up(t) % nj)),
            pl.BlockSpec((3, tn), lambda t: (0, up(t) % nj + nj)),
            pl.BlockSpec((tn, d), lambda t: (down(t) % nj, 0)),
            pl.BlockSpec((1, d), lambda t: (0, 0)),
        ],
        out_specs=pl.BlockSpec((tm, d), lambda t: (down(t) // nj, 0)),
        out_shape=jax.ShapeDtypeStruct((n, d), jnp.float32),
        scratch_shapes=[
            pltpu.VMEM((tm, d), jnp.bfloat16),
            pltpu.VMEM((tm + SUBLANES, tn), jnp.float32),
            pltpu.VMEM((tm + SUBLANES, tn), jnp.float32),
            pltpu.VMEM((nj, SUBLANES, tn), jnp.float32),
            pltpu.VMEM((nj, SUBLANES, tn), jnp.float32),
            pltpu.VMEM((tm, tn), jnp.bfloat16),
            pltpu.VMEM((tm, tn), jnp.bfloat16),
        ],
        compiler_params=pltpu.CompilerParams(
            dimension_semantics=("arbitrary",),
            vmem_limit_bytes=VMEM_LIMIT_BYTES),
        name="conv_ffn",
    )(x1, g, w_up, w_up, conv_w, conv_w, w_down, g_final)


def _block_diag_gates(wa, wx, heads_per_group):
    n_heads, hd, _ = wa.shape
    n_groups = n_heads // heads_per_group
    eye = jnp.eye(heads_per_group, dtype=wa.dtype)

    def bd(w):
        w = w.reshape(n_groups, heads_per_group, hd, hd)
        full = jnp.einsum('ghij,hk->ghikj', w, eye)
        return full.reshape(n_groups, heads_per_group * hd, heads_per_group * hd)

    return jnp.concatenate([bd(wa), bd(wx)], axis=-1)


def kernel(x, g_mix, w_in, lru_conv_w, lru_conv_b, lru_wa, lru_ba, lru_wx, lru_bx, lru_lambda, lru_w_out, sc_conv_w, sc_w_out, w_o, g_ffn, ffn_w_up, ffn_conv_w, ffn_w_down, g_final):
    bsz, seq, d = x.shape
    n = bsz * seq
    bf = jnp.bfloat16
    row = lambda v: v.reshape(1, -1)
    x2 = x.reshape(n, d)

    p0, w_in_bf = _in_proj_first(x2, row(g_mix), w_in, tm=1024, tn=768)
    p, (wlo, wso, wo) = _in_proj(
        x2, row(g_mix), w_in_bf, p0, [lru_w_out, sc_w_out, w_o], tm=1024, tn=1536)

    mxu_width = 256
    heads_per_group = max(1, mxu_width // lru_wa.shape[1])
    wg = _block_diag_gates(lru_wa, lru_wx, heads_per_group).astype(bf)
    x1, w_up, w_down = _mixer(p, x2, lru_conv_w, row(lru_conv_b), wg, row(lru_ba), row(lru_bx),
                              row(lru_lambda), wlo, sc_conv_w, wso, wo, [ffn_w_up, ffn_w_down],
                              tm=256, seq=seq)

    out = _conv_ffn(x1, row(g_ffn), w_up, ffn_conv_w, w_down, row(g_final),
                    tm=1024, tn=512, seq=seq)
    return out.reshape(bsz, seq, d)
```
